```python
import jax, jax.numpy as jnp
from jax import lax
import numpy as np

D_MODEL = 1024
BATCH = 16
SEQ = 256
DEPTH = 4
DEC_BATCH = 8
DEC_SEQ = 4096
PAST_LEN = 256

GRID_W = 64
D_MIX = D_MODEL
D_ATTN = D_MIX // 2
D_CONV = D_MIX - D_ATTN
HEAD_DIM = 64
N_HEADS = D_ATTN // HEAD_DIM
N_KV_HEADS = 2
N_REP = N_HEADS // N_KV_HEADS
D_KV = N_KV_HEADS * HEAD_DIM
D_IN = D_ATTN + 2 * D_KV + 2 * D_CONV
WINDOW = 128
BLOCK = 128
ATTN_SCALE = HEAD_DIM ** -0.5
ROPE_BASE = 10000.0
CONV_WIDTH = 31
CONV_GROUPS = 8
D_FF = 2816
N_EXPERTS = 8
TOP_K = 2
D_FF_EXPERT = 3584
N_DENSE = (DEPTH + 1) // 2
N_MOE = DEPTH // 2
EPS = 1e-6
NEG_INF = -1e30

kernel_name = "hybrid_prefix_diffusion_step"


def rmsnorm(x, g):
    xf = x.astype(jnp.float32)
    y = xf * lax.rsqrt(jnp.mean(xf * xf, axis=-1, keepdims=True) + EPS)
    return (y * g.astype(jnp.float32)).astype(x.dtype)


def axial_rope(x):
    L = x.shape[1]
    rows = L // GRID_W
    row = jnp.repeat(jnp.arange(rows), GRID_W)
    col = jnp.tile(jnp.arange(GRID_W), rows)
    half = HEAD_DIM // 2
    n_freq = half // 2
    inv = ROPE_BASE ** (-jnp.arange(n_freq, dtype=jnp.float32) * 2.0 / half)

    def rot(xa, pos):
        ang = pos.astype(jnp.float32)[:, None] * inv[None, :]
        cos = jnp.cos(ang)[:, None, :].astype(x.dtype)
        sin = jnp.sin(ang)[:, None, :].astype(x.dtype)
        x1, x2 = xa[..., :n_freq], xa[..., n_freq:]
        return jnp.concatenate([x1 * cos - x2 * sin, x1 * sin + x2 * cos], axis=-1)

    return jnp.concatenate([rot(x[..., :half], row), rot(x[..., half:], col)], axis=-1)


def qk_scores(q, k):
    return jnp.einsum('bqhgd,bkhd->bhgqk', q * ATTN_SCALE, k).astype(jnp.float32)


def sink_attend(sink, scores, values):
    s0 = scores[0]
    s_sink = jnp.broadcast_to(sink.astype(jnp.float32)[None, :, :, None, None], s0.shape[:-1] + (1,))
    p = jax.nn.softmax(jnp.concatenate([s_sink] + scores, axis=-1), axis=-1)
    out = None
    off = 1
    for s, v in zip(scores, values):
        n = s.shape[-1]
        pv = jnp.einsum('bhgqk,bkhd->bqhgd', p[..., off:off + n].astype(v.dtype), v)
        out = pv if out is None else out + pv
        off += n
    return out


def context_attention(q, k, v, sink):
    B, S = q.shape[0], q.shape[1]
    nb = S // BLOCK
    qb = jnp.moveaxis(q.reshape(B, nb, BLOCK, N_KV_HEADS, N_REP, HEAD_DIM), 1, 0)
    sk = sink.reshape(N_KV_HEADS, N_REP)
    out = lax.map(lambda qblk: sink_attend(sk, [qk_scores(qblk, k)], [v]), qb)
    return jnp.moveaxis(out, 0, 1).reshape(B, S, D_ATTN)


def latent_attention(q, k, v, ck, cv, sink):
    B, L = q.shape[0], q.shape[1]
    nb = L // BLOCK
    q = axial_rope(q)
    k = axial_rope(k)
    kp = jnp.pad(k, ((0, 0), (BLOCK, BLOCK), (0, 0), (0, 0)))
    vp = jnp.pad(v, ((0, 0), (BLOCK, BLOCK), (0, 0), (0, 0)))
    qb = jnp.moveaxis(q.reshape(B, nb, BLOCK, N_KV_HEADS, N_REP, HEAD_DIM), 1, 0)
    sk = sink.reshape(N_KV_HEADS, N_REP)
    i = jnp.arange(BLOCK)[:, None]
    j = jnp.arange(3 * BLOCK)[None, :]
    band = jnp.abs(j - BLOCK - i) <= WINDOW

    def one(args):
        b, qblk = args
        kw = lax.dynamic_slice_in_dim(kp, b * BLOCK, 3 * BLOCK, axis=1)
        vw = lax.dynamic_slice_in_dim(vp, b * BLOCK, 3 * BLOCK, axis=1)
        kpos = b * BLOCK - BLOCK + jnp.arange(3 * BLOCK)
        valid = band & ((kpos >= 0) & (kpos < L))[None, :]
        s_win = jnp.where(valid, qk_scores(qblk, kw), NEG_INF)
        s_ctx = qk_scores(qblk, ck)
        return sink_attend(sk, [s_ctx, s_win], [cv, vw])

    out = lax.map(one, (jnp.arange(nb), qb))
    return jnp.moveaxis(out, 0, 1).reshape(B, L, D_ATTN)


def conv_module(u, w_dw, b_dw, g_n, b_n):
    a, gt = jnp.split(u, 2, axis=-1)
    z = a * jax.nn.sigmoid(gt)
    z = lax.conv_general_dilated(z, w_dw[:, None, :].astype(z.dtype), window_strides=(1,),
                                 padding=[(CONV_WIDTH // 2, CONV_WIDTH // 2)],
                                 dimension_numbers=('NWC', 'WIO', 'NWC'),
                                 feature_group_count=D_CONV) + b_dw
    B, L, _ = z.shape
    zf = z.astype(jnp.float32).reshape(B, L, CONV_GROUPS, D_CONV // CONV_GROUPS)
    mu = jnp.mean(zf, axis=-1, keepdims=True)
    var = jnp.mean(jnp.square(zf - mu), axis=-1, keepdims=True)
    zn = ((zf - mu) * lax.rsqrt(var + EPS)).reshape(B, L, D_CONV)
    zn = zn * g_n.astype(jnp.float32) + b_n.astype(jnp.float32)
    return jax.nn.silu(zn).astype(u.dtype)


def swiglu(h, w_gu, w_down):
    g, up = jnp.split(h @ w_gu, 2, axis=-1)
    return (jax.nn.silu(g) * up) @ w_down


def moe_swiglu(h, w_r, b_r, w_gu, w_down):
    logits = (h @ w_r).astype(jnp.float32) + b_r.astype(jnp.float32)
    top_v, top_i = lax.top_k(logits, TOP_K)
    gates = jax.nn.softmax(top_v, axis=-1)
    combine = jnp.sum(jax.nn.one_hot(top_i, N_EXPERTS, dtype=jnp.float32) * gates[..., None], axis=-2).astype(h.dtype)
    out = jnp.zeros_like(h)
    for e in range(N_EXPERTS):
        out = out + combine[..., e:e + 1] * swiglu(h, w_gu[e], w_down[e])
    return out


def setup_inputs(seed: int = 0) -> dict:
    key = jax.random.key(seed)
    ks = jax.random.split(key, 26)
    f32 = jnp.float32

    def nrm(k, shape, scale):
        return jax.random.normal(k, shape, f32) * scale

    def gain(k, shape):
        return 1.0 + 0.05 * jax.random.normal(k, shape, f32)

    return {
        "x_prompt": nrm(ks[0], (BATCH, SEQ, D_MODEL), 1.0),
        "x_sample": nrm(ks[1], (DEC_BATCH, DEC_SEQ, D_MODEL), 1.0),
        "c": nrm(ks[2], (DEC_BATCH, D_MODEL), 1.0),
        "c_ctx": nrm(ks[3], (D_MODEL,), 1.0),
        "cache_k": nrm(ks[4], (DEC_BATCH, DEPTH, PAST_LEN, N_KV_HEADS, HEAD_DIM), 1.0),
        "cache_v": nrm(ks[5], (DEC_BATCH, DEPTH, PAST_LEN, N_KV_HEADS, HEAD_DIM), 1.0),
        "w_mod": nrm(ks[6], (DEPTH, D_MODEL, 6 * D_MODEL), 0.5 * D_MODEL ** -0.5),
        "b_mod": nrm(ks[7], (DEPTH, 6 * D_MODEL), 0.02),
        "g_pre_mix": gain(ks[8], (DEPTH, D_MODEL)),
        "g_post_mix": gain(ks[9], (DEPTH, D_MODEL)),
        "g_pre_ffn": gain(ks[10], (DEPTH, D_MODEL)),
        "g_post_ffn": gain(ks[11], (DEPTH, D_MODEL)),
        "w_in": nrm(ks[12], (DEPTH, D_MODEL, D_IN), D_MODEL ** -0.5),
        "w_out": nrm(ks[13], (DEPTH, D_MIX, D_MODEL), D_MIX ** -0.5),
        "attn_sink": nrm(ks[14], (DEPTH, N_HEADS), 0.5),
        "conv_w": nrm(ks[15], (DEPTH, CONV_WIDTH, D_CONV), CONV_WIDTH ** -0.5),
        "conv_b": nrm(ks[16], (DEPTH, D_CONV), 0.02),
        "conv_norm_g": gain(ks[17], (DEPTH, D_CONV)),
        "conv_norm_b": nrm(ks[18], (DEPTH, D_CONV), 0.02),
        "ffn_w_gu": nrm(ks[19], (N_DENSE, D_MODEL, 2 * D_FF), D_MODEL ** -0.5),
        "ffn_w_down": nrm(ks[20], (N_DENSE, D_FF, D_MODEL), D_FF ** -0.5),
        "moe_w_router": nrm(ks[21], (N_MOE, D_MODEL, N_EXPERTS), D_MODEL ** -0.5),
        "moe_b_router": nrm(ks[22], (N_MOE, N_EXPERTS), 0.01),
        "moe_w_gu": nrm(ks[23], (N_MOE, N_EXPERTS, D_MODEL, 2 * D_FF_EXPERT), D_MODEL ** -0.5),
        "moe_w_down": nrm(ks[24], (N_MOE, N_EXPERTS, D_FF_EXPERT, D_MODEL), D_FF_EXPERT ** -0.5),
    }


def reference(x_prompt, x_sample, c, c_ctx, cache_k, cache_v, w_mod, b_mod,
              g_pre_mix, g_post_mix, g_pre_ffn, g_post_ffn, w_in, w_out, attn_sink,
              conv_w, conv_b, conv_norm_g, conv_norm_b, ffn_w_gu, ffn_w_down,
              moe_w_router, moe_b_router, moe_w_gu, moe_w_down):

    def layer(l, x, mod, attend):
        shift1, scale1, gate1, shift2, scale2, gate2 = jnp.split(mod, 6, axis=-1)
        B, L, _ = x.shape
        h = rmsnorm(x, g_pre_mix[l]) * (1 + scale1) + shift1
        q, k, v, u = jnp.split(h @ w_in[l], [D_ATTN, D_ATTN + D_KV, D_ATTN + 2 * D_KV], axis=-1)
        q = q.reshape(B, L, N_HEADS, HEAD_DIM)
        k = k.reshape(B, L, N_KV_HEADS, HEAD_DIM)
        v = v.reshape(B, L, N_KV_HEADS, HEAD_DIM)
        a = attend(q, k, v, attn_sink[l])
        cm = conv_module(u, conv_w[l], conv_b[l], conv_norm_g[l], conv_norm_b[l])
        m = jnp.concatenate([a, cm], axis=-1) @ w_out[l]
        x = x + gate1 * rmsnorm(m, g_post_mix[l])
        h = rmsnorm(x, g_pre_ffn[l]) * (1 + scale2) + shift2
        if l % 2 == 0:
            f = swiglu(h, ffn_w_gu[l // 2], ffn_w_down[l // 2])
        else:
            f = moe_swiglu(h, moe_w_router[l // 2], moe_b_router[l // 2], moe_w_gu[l // 2], moe_w_down[l // 2])
        x = x + gate2 * rmsnorm(f, g_post_ffn[l])
        return x, k, v

    def ctx_attend(q, k, v, sink):
        B, L = q.shape[0], q.shape[1]
        return context_attention(q.reshape(B, L, N_KV_HEADS, N_REP, HEAD_DIM), k, v, sink)

    xp = x_prompt
    ks_list = []
    vs_list = []
    for l in range(DEPTH):
        mod_ctx = (jax.nn.silu(c_ctx) @ w_mod[l] + b_mod[l])[None, None, :]
        xp, k_l, v_l = layer(l, xp, mod_ctx, ctx_attend)
        ks_list.append(k_l)
        vs_list.append(v_l)
    new_cache_k = jnp.stack(ks_list, axis=1)
    new_cache_v = jnp.stack(vs_list, axis=1)

    xs = x_sample
    for l in range(DEPTH):
        mod_lat = (jax.nn.silu(c) @ w_mod[l] + b_mod[l])[:, None, :]
        ck = cache_k[:, l]
        cv = cache_v[:, l]

        def lat_attend(q, k, v, sink, ck=ck, cv=cv):
            return latent_attention(q, k, v, ck, cv, sink)

        xs, _, _ = layer(l, xs, mod_lat, lat_attend)

    return (xp, xs, new_cache_k, new_cache_v)
```

```python
import functools

import jax
import jax.numpy as jnp
from jax import lax
from jax.experimental import pallas as pl
from jax.experimental.pallas import tpu as pltpu

F32 = jnp.float32
BF16 = jnp.bfloat16

D_MODEL = 1024
DEPTH = 4
GRID_W = 64
D_ATTN = 512
D_CONV = 512
HEAD_DIM = 64
N_HEADS = 8
N_KV_HEADS = 2
N_REP = N_HEADS // N_KV_HEADS
D_KV = N_KV_HEADS * HEAD_DIM
D_IN = D_ATTN + 2 * D_KV + 2 * D_CONV
WINDOW = 128
BLOCK = 128
ATTN_SCALE = HEAD_DIM ** -0.5
ROPE_BASE = 10000.0
CONV_WIDTH = 31
CONV_HALF = CONV_WIDTH // 2
CONV_GROUPS = 8
D_FF = 2816
N_EXPERTS = 8
D_FF_EXPERT = 3584
EPS = 1e-6
NEG_INF = -1e30

LANES = 128
SUBLANES = 8
MXU_DIM = 256
VMEM_LIMIT = 48 * 1024 * 1024

TOKEN_TILE = 512
CONV_TILE = 256
CONV_HALO = 16
CONV_ROWS = 128
FF_CHUNK = 256
N_FF_CHUNKS = D_FF // FF_CHUNK
MOE_TILE = 512
MOE_CHUNK = 512
N_MOE_CHUNKS = D_FF_EXPERT // MOE_CHUNK
MOD_ROWS = 16
ROUTE_LANES = 128


def _params(*sem):
    return pltpu.CompilerParams(dimension_semantics=sem, vmem_limit_bytes=VMEM_LIMIT)


def _silu(x):
    return x * jax.nn.sigmoid(x)


def _rms(x, g):
    return x * lax.rsqrt(jnp.mean(x * x, axis=-1, keepdims=True) + EPS) * g


def _split_bf16(x):
    hi = x.astype(BF16)
    lo = (x - hi.astype(F32)).astype(BF16)
    return hi, lo


def _mod_kernel(c_ref, w_ref, b_ref, o_ref):
    a = _silu(c_ref[...]).astype(BF16)
    o_ref[...] = jnp.dot(a, w_ref[...].astype(BF16), preferred_element_type=F32) + b_ref[...]


def _modulations(cvec, w_mod, b_mod):
    out = pl.pallas_call(
        _mod_kernel,
        grid=(DEPTH, 6),
        in_specs=[
            pl.BlockSpec((MOD_ROWS, D_MODEL), lambda l, s: (0, 0)),
            pl.BlockSpec((None, D_MODEL, D_MODEL), lambda l, s: (l, 0, s)),
            pl.BlockSpec((None, 1, D_MODEL), lambda l, s: (l, 0, s)),
        ],
        out_specs=pl.BlockSpec((None, MOD_ROWS, D_MODEL), lambda l, s: (l, 0, s)),
        out_shape=jax.ShapeDtypeStruct((DEPTH, MOD_ROWS, 6 * D_MODEL), F32),
        compiler_params=_params("arbitrary", "arbitrary"),
        name="modulations",
    )(cvec, w_mod, b_mod.reshape(DEPTH, 1, 6 * D_MODEL))
    return out.reshape(DEPTH, MOD_ROWS, 6, D_MODEL)


def _rope(x, cos, sin):
    lane = lax.broadcasted_iota(jnp.int32, x.shape, 1)
    width = x.shape[1]
    partner = jnp.where((lane & 16) == 0, pltpu.roll(x, width - 16, 1), pltpu.roll(x, 16, 1))
    return x * cos + partner * sin


def _inproj_kernel(*refs, rope):
    if rope:
        x_ref, mod_ref, g_ref, w_ref, cos_ref, sin_ref, q_ref, k_ref, v_ref, z_ref = refs
    else:
        x_ref, mod_ref, g_ref, w_ref, q_ref, k_ref, v_ref, z_ref = refs
    h = _rms(x_ref[...], g_ref[...]) * (1.0 + mod_ref[1:2, :]) + mod_ref[0:1, :]
    p = jnp.dot(h.astype(BF16), w_ref[...], preferred_element_type=F32)
    q = p[:, :D_ATTN]
    k = p[:, D_ATTN:D_ATTN + D_KV]
    if rope:
        cos = cos_ref[...]
        sin = sin_ref[...]
        q = _rope(q, cos, sin)
        k = _rope(k, cos[:, :D_KV], sin[:, :D_KV])
    q_ref[...] = (q * ATTN_SCALE).astype(BF16)
    k_ref[...] = k
    v_ref[...] = p[:, D_ATTN + D_KV:D_ATTN + 2 * D_KV]
    a = p[:, D_ATTN + 2 * D_KV:D_ATTN + 2 * D_KV + D_CONV]
    gt = p[:, D_ATTN + 2 * D_KV + D_CONV:]
    z_ref[...] = (a * jax.nn.sigmoid(gt)).astype(BF16)


def _inproj(x, mods, g, w_in, rope_tabs, seq_len):
    n = x.shape[0]
    tm = TOKEN_TILE
    rows_per_mod = n // mods.shape[0]
    tiles_per_mod = rows_per_mod // tm
    row = lambda i: (i, 0)
    const = lambda i: (0, 0)
    in_specs = [
        pl.BlockSpec((tm, D_MODEL), row),
        pl.BlockSpec((None, 6, D_MODEL), lambda i: (i // tiles_per_mod, 0, 0)),
        pl.BlockSpec((1, D_MODEL), const),
        pl.BlockSpec((D_MODEL, D_IN), const),
    ]
    args = [x, mods, g, w_in]
    if rope_tabs is not None:
        tiles_per_seq = seq_len // tm
        tab = pl.BlockSpec((tm, D_ATTN), lambda i: (i % tiles_per_seq, 0))
        in_specs += [tab, tab]
        args += list(rope_tabs)
    return pl.pallas_call(
        functools.partial(_inproj_kernel, rope=rope_tabs is not None),
        grid=(n // tm,),
        in_specs=in_specs,
        out_specs=[
            pl.BlockSpec((tm, D_ATTN), row),
            pl.BlockSpec((tm, D_KV), row),
            pl.BlockSpec((tm, D_KV), row),
            pl.BlockSpec((tm, D_CONV), row),
        ],
        out_shape=[
            jax.ShapeDtypeStruct((n, D_ATTN), BF16),
            jax.ShapeDtypeStruct((n, D_KV), F32),
            jax.ShapeDtypeStruct((n, D_KV), F32),
            jax.ShapeDtypeStruct((n, D_CONV), BF16),
        ],
        compiler_params=_params("arbitrary"),
        name="inproj",
    )(*args)


def _rope_tables(seq_len):
    half = HEAD_DIM // 2
    n_freq = half // 2
    t = jnp.arange(seq_len)
    inv = ROPE_BASE ** (-jnp.arange(n_freq, dtype=F32) * 2.0 / half)
    ang_r = (t // GRID_W).astype(F32)[:, None] * inv[None, :]
    ang_c = (t % GRID_W).astype(F32)[:, None] * inv[None, :]
    cos = jnp.concatenate([jnp.cos(ang_r)] * 2 + [jnp.cos(ang_c)] * 2, axis=-1)
    sin = jnp.concatenate([-jnp.sin(ang_r), jnp.sin(ang_r), -jnp.sin(ang_c), jnp.sin(ang_c)], axis=-1)
    return jnp.tile(cos, (1, N_HEADS)), jnp.tile(sin, (1, N_HEADS))


def _attn_kernel(*refs, windowed, seq_len):
    if windowed:
        sink_ref, q_ref, ck_ref, cv_ref, kp_ref, kc_ref, kn_ref, vp_ref, vc_ref, vn_ref, o_ref = refs
    else:
        sink_ref, q_ref, ck_ref, cv_ref, o_ref = refs
    qb = q_ref.shape[0]
    q = q_ref[...]
    rows = N_REP * qb
    if windowed:
        j = pl.program_id(1)
        qi = lax.broadcasted_iota(jnp.int32, (rows, 1), 0) & (qb - 1)
        col = lax.broadcasted_iota(jnp.int32, (rows, 3 * BLOCK), 1)
        lower = jnp.maximum(qi, BLOCK - j * BLOCK)
        upper = jnp.minimum(qi + 2 * WINDOW, seq_len + BLOCK - 1 - j * BLOCK)
        valid = (col >= lower) & (col <= upper)
        kw = jnp.concatenate([kp_ref[...], kc_ref[...], kn_ref[...]], axis=0).astype(BF16)
        vw = jnp.concatenate([vp_ref[...], vc_ref[...], vn_ref[...]], axis=0).astype(BF16)
    ck = ck_ref[...].astype(BF16)
    cv = cv_ref[...].astype(BF16)
    nt = (((1,), (1,)), ((), ()))
    outs = []
    for h in range(N_KV_HEADS):
        hs = slice(h * HEAD_DIM, (h + 1) * HEAD_DIM)
        qs = jnp.concatenate(
            [q[:, (N_REP * h + g) * HEAD_DIM:(N_REP * h + g + 1) * HEAD_DIM] for g in range(N_REP)],
            axis=0)
        sink = jnp.concatenate(
            [jnp.full((qb, 1), sink_ref[N_REP * h + g], F32) for g in range(N_REP)], axis=0)
        s_ctx = lax.dot_general(qs, ck[:, hs], nt, preferred_element_type=F32)
        m = jnp.maximum(sink, jnp.max(s_ctx, axis=-1, keepdims=True))
        if windowed:
            s_win = lax.dot_general(qs, kw[:, hs], nt, preferred_element_type=F32)
            s_win = jnp.where(valid, s_win, NEG_INF)
            m = jnp.maximum(m, jnp.max(s_win, axis=-1, keepdims=True))
        p_ctx = jnp.exp(s_ctx - m)
        denom = jnp.exp(sink - m) + jnp.sum(p_ctx, axis=-1, keepdims=True)
        acc = jnp.dot(p_ctx.astype(BF16), cv[:, hs], preferred_element_type=F32)
        if windowed:
            p_win = jnp.exp(s_win - m)
            denom = denom + jnp.sum(p_win, axis=-1, keepdims=True)
            acc = acc + jnp.dot(p_win.astype(BF16), vw[:, hs], preferred_element_type=F32)
        o = acc * (1.0 / denom)
        outs += [o[g * qb:(g + 1) * qb, :] for g in range(N_REP)]
    o_ref[...] = jnp.concatenate(outs, axis=-1).astype(BF16)


def _attention_latent(q, k, v, cache_k, cache_v, sink, layer, batch, seq_len):
    nb = seq_len // BLOCK
    q3 = q.reshape(batch, seq_len, D_ATTN)
    k3 = k.reshape(batch, seq_len, D_KV)
    v3 = v.reshape(batch, seq_len, D_KV)
    past = cache_k.shape[2]
    blk = lambda f: pl.BlockSpec((None, BLOCK, D_KV), f)
    prev_map = lambda b, j: (b, jnp.maximum(j - 1, 0), 0)
    cur_map = lambda b, j: (b, j, 0)
    next_map = lambda b, j: (b, jnp.minimum(j + 1, nb - 1), 0)
    cache_spec = pl.BlockSpec((None, None, past, D_KV), lambda b, j: (b, layer, 0, 0))
    out = pl.pallas_call(
        functools.partial(_attn_kernel, windowed=True, seq_len=seq_len),
        grid=(batch, nb),
        in_specs=[
            pl.BlockSpec(memory_space=pltpu.SMEM),
            pl.BlockSpec((None, BLOCK, D_ATTN), cur_map),
            cache_spec, cache_spec,
            blk(prev_map), blk(cur_map), blk(next_map),
            blk(prev_map), blk(cur_map), blk(next_map),
        ],
        out_specs=pl.BlockSpec((None, BLOCK, D_ATTN), cur_map),
        out_shape=jax.ShapeDtypeStruct((batch, seq_len, D_ATTN), BF16),
        compiler_params=_params("arbitrary", "arbitrary"),
        name="attn_latent",
    )(sink, q3, cache_k, cache_v, k3, k3, k3, v3, v3, v3)
    return out.reshape(batch * seq_len, D_ATTN)


def _attention_context(q, k, v, sink, batch, seq_len):
    q3 = q.reshape(batch, seq_len, D_ATTN)
    k3 = k.reshape(batch, seq_len, D_KV)
    v3 = v.reshape(batch, seq_len, D_KV)
    seq = lambda b: (b, 0, 0)
    out = pl.pallas_call(
        functools.partial(_attn_kernel, windowed=False, seq_len=seq_len),
        grid=(batch,),
        in_specs=[
            pl.BlockSpec(memory_space=pltpu.SMEM),
            pl.BlockSpec((None, seq_len, D_ATTN), seq),
            pl.BlockSpec((None, seq_len, D_KV), seq),
            pl.BlockSpec((None, seq_len, D_KV), seq),
        ],
        out_specs=pl.BlockSpec((None, seq_len, D_ATTN), seq),
        out_shape=jax.ShapeDtypeStruct((batch, seq_len, D_ATTN), BF16),
        compiler_params=_params("arbitrary"),
        name="attn_context",
    )(sink, q3, k3, v3)
    return out.reshape(batch * seq_len, D_ATTN)


def _conv_kernel(z_ref, zp_ref, zn_ref, w_ref, b_ref, gavg_ref, gn_ref, bn_ref, o_ref, ext_ref, y_ref,
                 *, tiles_per_seq):
    i = pl.program_id(0)
    tt = z_ref.shape[0]
    first = (i % tiles_per_seq) == 0
    last = (i % tiles_per_seq) == tiles_per_seq - 1
    ext_ref[0:CONV_HALO, :] = jnp.where(first, 0.0, zp_ref[...].astype(F32))
    ext_ref[CONV_HALO:CONV_HALO + tt, :] = z_ref[...].astype(F32)
    ext_ref[CONV_HALO + tt:, :] = jnp.where(last, 0.0, zn_ref[...].astype(F32))
    base = CONV_HALO - CONV_HALF
    span = CONV_ROWS + SUBLANES * ((CONV_WIDTH - 1) // SUBLANES)
    for c in range(D_CONV // LANES):
        cs = slice(c * LANES, (c + 1) * LANES)
        w = w_ref[:, cs]
        for rb in range(tt // CONV_ROWS):
            r0 = rb * CONV_ROWS
            acc = jnp.broadcast_to(b_ref[:, cs], (CONV_ROWS, LANES))
            for r in range(SUBLANES):
                zr = ext_ref[r0 + base + r:r0 + base + r + span, cs]
                for j in range(r, CONV_WIDTH, SUBLANES):
                    acc = acc + w[j:j + 1, :] * zr[j - r:j - r + CONV_ROWS, :]
            y_ref[r0:r0 + CONV_ROWS, cs] = acc
    y = y_ref[...]
    gavg = gavg_ref[...]

    def group_mean(a):
        hi, lo = _split_bf16(a)
        return (jnp.dot(hi, gavg, preferred_element_type=F32)
                + jnp.dot(lo, gavg, preferred_element_type=F32))

    d = y - group_mean(y)
    var = group_mean(d * d)
    zn = d * lax.rsqrt(var + EPS) * gn_ref[...] + bn_ref[...]
    o_ref[...] = _silu(zn).astype(BF16)


def _conv_module(z, conv_w, conv_b, gn, bn, gavg, seq_len):
    n = z.shape[0]
    tt = CONV_TILE
    halo_per_tile = tt // CONV_HALO
    n_halo = n // CONV_HALO
    const = lambda i: (0, 0)
    return pl.pallas_call(
        functools.partial(_conv_kernel, tiles_per_seq=seq_len // tt),
        grid=(n // tt,),
        in_specs=[
            pl.BlockSpec((tt, D_CONV), lambda i: (i, 0)),
            pl.BlockSpec((CONV_HALO, D_CONV), lambda i: (jnp.maximum(i * halo_per_tile - 1, 0), 0)),
            pl.BlockSpec((CONV_HALO, D_CONV),
                         lambda i: (jnp.minimum((i + 1) * halo_per_tile, n_halo - 1), 0)),
            pl.BlockSpec((CONV_WIDTH, D_CONV), const),
            pl.BlockSpec((1, D_CONV), const),
            pl.BlockSpec((D_CONV, D_CONV), const),
            pl.BlockSpec((1, D_CONV), const),
            pl.BlockSpec((1, D_CONV), const),
        ],
        out_specs=pl.BlockSpec((tt, D_CONV), lambda i: (i, 0)),
        out_shape=jax.ShapeDtypeStruct((n, D_CONV), BF16),
        scratch_shapes=[
            pltpu.VMEM((tt + 2 * CONV_HALO, D_CONV), F32),
            pltpu.VMEM((tt, D_CONV), F32),
        ],
        compiler_params=_params("arbitrary"),
        name="conv_module",
    )(z, z, z, conv_w, conv_b, gavg, gn, bn)


def _outproj_kernel(*refs, routed):
    if routed:
        (a_ref, cm_ref, x_ref, mod_ref, gpost_ref, gpre_ref, w_ref, wr_ref, br_ref,
         x1_ref, h_ref, route_ref) = refs
    else:
        a_ref, cm_ref, x_ref, mod_ref, gpost_ref, gpre_ref, w_ref, x1_ref, h_ref = refs
    m = (jnp.dot(a_ref[...], w_ref[:D_ATTN, :], preferred_element_type=F32)
         + jnp.dot(cm_ref[...], w_ref[D_ATTN:, :], preferred_element_type=F32))
    x1 = x_ref[...] + mod_ref[2:3, :] * _rms(m, gpost_ref[...])
    x1_ref[...] = x1
    h = _rms(x1, gpre_ref[...]) * (1.0 + mod_ref[4:5, :]) + mod_ref[3:4, :]
    h_ref[...] = h.astype(BF16)
    if routed:
        h_hi, h_lo = _split_bf16(h)
        w_hi, w_lo = _split_bf16(wr_ref[...])
        logits = (jnp.dot(h_hi, w_hi, preferred_element_type=F32)
                  + jnp.dot(h_hi, w_lo, preferred_element_type=F32)
                  + jnp.dot(h_lo, w_hi, preferred_element_type=F32)) + br_ref[...]
        lane = lax.broadcasted_iota(jnp.int32, logits.shape, 1).astype(F32)
        logits = jnp.where(lane < N_EXPERTS, logits, -jnp.inf)
        m1 = jnp.max(logits, axis=-1, keepdims=True)
        i1 = jnp.min(jnp.where(logits == m1, lane, float(ROUTE_LANES)), axis=-1, keepdims=True)
        rest = jnp.where(lane == i1, -jnp.inf, logits)
        m2 = jnp.max(rest, axis=-1, keepdims=True)
        i2 = jnp.min(jnp.where(rest == m2, lane, float(ROUTE_LANES)), axis=-1, keepdims=True)
        e2 = jnp.exp(m2 - m1)
        g1 = 1.0 / (1.0 + e2)
        g2 = e2 * g1
        route_ref[...] = jnp.where(lane == 0, i1, jnp.where(lane == 1, i2,
                                   jnp.where(lane == 2, g1, jnp.where(lane == 3, g2, 0.0))))


def _outproj(a, cm, x, mods, gpost, gpre, w_out, router):
    n = x.shape[0]
    tm = TOKEN_TILE
    tiles_per_mod = (n // mods.shape[0]) // tm
    row = lambda i: (i, 0)
    const = lambda i: (0, 0)
    in_specs = [
        pl.BlockSpec((tm, D_ATTN), row),
        pl.BlockSpec((tm, D_CONV), row),
        pl.BlockSpec((tm, D_MODEL), row),
        pl.BlockSpec((None, 6, D_MODEL), lambda i: (i // tiles_per_mod, 0, 0)),
        pl.BlockSpec((1, D_MODEL), const),
        pl.BlockSpec((1, D_MODEL), const),
        pl.BlockSpec((D_MODEL, D_MODEL), const),
    ]
    args = [a, cm, x, mods, gpost, gpre, w_out]
    out_specs = [pl.BlockSpec((tm, D_MODEL), row), pl.BlockSpec((tm, D_MODEL), row)]
    out_shape = [jax.ShapeDtypeStruct((n, D_MODEL), F32), jax.ShapeDtypeStruct((n, D_MODEL), BF16)]
    if router is not None:
        in_specs += [pl.BlockSpec((D_MODEL, ROUTE_LANES), const), pl.BlockSpec((1, ROUTE_LANES), const)]
        args += list(router)
        out_specs.append(pl.BlockSpec((tm, ROUTE_LANES), row))
        out_shape.append(jax.ShapeDtypeStruct((n, ROUTE_LANES), F32))
    return pl.pallas_call(
        functools.partial(_outproj_kernel, routed=router is not None),
        grid=(n // tm,),
        in_specs=in_specs,
        out_specs=out_specs,
        out_shape=out_shape,
        compiler_params=_params("arbitrary"),
        name="outproj",
    )(*args)


def _ffn_kernel(h_ref, x_ref, mod_ref, g_ref, wg_ref, wu_ref, wd_ref, o_ref, acc_ref):
    h = h_ref[...]

    def chunk(c, carry):
        g = jnp.dot(h, wg_ref[c], preferred_element_type=F32)
        u = jnp.dot(h, wu_ref[c], preferred_element_type=F32)
        part = jnp.dot((_silu(g) * u).astype(BF16), wd_ref[c], preferred_element_type=F32)

        @pl.when(c == 0)
        def _():
            acc_ref[...] = part

        @pl.when(c > 0)
        def _():
            acc_ref[...] += part

        return carry

    lax.fori_loop(0, N_FF_CHUNKS, chunk, 0)
    o_ref[...] = x_ref[...] + mod_ref[5:6, :] * _rms(acc_ref[...], g_ref[...])


def _ffn_dense(h, x, mods, g, wg, wu, wd):
    n = x.shape[0]
    tm = TOKEN_TILE
    tiles_per_mod = (n // mods.shape[0]) // tm
    row = lambda i: (i, 0)
    whole = lambda i: (0, 0, 0)
    resident = pl.Buffered(1)
    return pl.pallas_call(
        _ffn_kernel,
        grid=(n // tm,),
        in_specs=[
            pl.BlockSpec((tm, D_MODEL), row),
            pl.BlockSpec((tm, D_MODEL), row),
            pl.BlockSpec((None, 6, D_MODEL), lambda i: (i // tiles_per_mod, 0, 0)),
            pl.BlockSpec((1, D_MODEL), lambda i: (0, 0)),
            pl.BlockSpec((N_FF_CHUNKS, D_MODEL, FF_CHUNK), whole, pipeline_mode=resident),
            pl.BlockSpec((N_FF_CHUNKS, D_MODEL, FF_CHUNK), whole, pipeline_mode=resident),
            pl.BlockSpec((N_FF_CHUNKS, FF_CHUNK, D_MODEL), whole, pipeline_mode=resident),
        ],
        out_specs=pl.BlockSpec((tm, D_MODEL), row),
        out_shape=jax.ShapeDtypeStruct((n, D_MODEL), F32),
        scratch_shapes=[pltpu.VMEM((tm, D_MODEL), F32)],
        compiler_params=_params("arbitrary"),
        name="ffn_dense",
    )(h, x, mods, g, wg, wu, wd)


def _moe_kernel(te_ref, na_ref, h_ref, wg_ref, wu_ref, wd_ref, o_ref, acc_ref):
    i = pl.program_id(0)
    j = pl.program_id(1)

    @pl.when(i < na_ref[0])
    def _():
        h = h_ref[...]
        g = jnp.dot(h, wg_ref[...], preferred_element_type=F32)
        u = jnp.dot(h, wu_ref[...], preferred_element_type=F32)
        part = jnp.dot((_silu(g) * u).astype(BF16), wd_ref[...], preferred_element_type=F32)

        @pl.when(j == 0)
        def _():
            acc_ref[...] = part

        @pl.when(j > 0)
        def _():
            acc_ref[...] += part

        @pl.when(j == N_MOE_CHUNKS - 1)
        def _():
            o_ref[...] = acc_ref[...]


def _moe_experts(hs, tile_expert, n_active, w_gu, w_down):
    rows = hs.shape[0]
    tm = MOE_TILE

    def tile(i, na):
        return jnp.minimum(i, na[0] - 1)

    def chunk(i, j, na):
        return jnp.where(i < na[0], j, N_MOE_CHUNKS - 1)

    grid_spec = pltpu.PrefetchScalarGridSpec(
        num_scalar_prefetch=2,
        grid=(rows // tm, N_MOE_CHUNKS),
        in_specs=[
            pl.BlockSpec((tm, D_MODEL), lambda i, j, te, na: (tile(i, na), 0)),
            pl.BlockSpec((None, D_MODEL, MOE_CHUNK),
                         lambda i, j, te, na: (te[tile(i, na)], 0, chunk(i, j, na))),
            pl.BlockSpec((None, D_MODEL, MOE_CHUNK),
                         lambda i, j, te, na: (te[tile(i, na)], 0, N_MOE_CHUNKS + chunk(i, j, na))),
            pl.BlockSpec((None, MOE_CHUNK, D_MODEL),
                         lambda i, j, te, na: (te[tile(i, na)], chunk(i, j, na), 0)),
        ],
        out_specs=pl.BlockSpec((tm, D_MODEL), lambda i, j, te, na: (tile(i, na), 0)),
        scratch_shapes=[pltpu.VMEM((tm, D_MODEL), F32)],
    )
    return pl.pallas_call(
        _moe_kernel,
        grid_spec=grid_spec,
        out_shape=jax.ShapeDtypeStruct((rows, D_MODEL), F32),
        compiler_params=_params("arbitrary", "arbitrary"),
        name="moe_experts",
    )(tile_expert, n_active, hs, w_gu, w_gu, w_down)


def _combine_kernel(y1_ref, y2_ref, route_ref, x_ref, mod_ref, g_ref, o_ref):
    f = route_ref[:, 2:3] * y1_ref[...] + route_ref[:, 3:4] * y2_ref[...]
    o_ref[...] = x_ref[...] + mod_ref[5:6, :] * _rms(f, g_ref[...])


def _moe_combine(y1, y2, route, x, mods, g):
    n = x.shape[0]
    tm = TOKEN_TILE
    tiles_per_mod = (n // mods.shape[0]) // tm
    row = lambda i: (i, 0)
    return pl.pallas_call(
        _combine_kernel,
        grid=(n // tm,),
        in_specs=[
            pl.BlockSpec((tm, D_MODEL), row),
            pl.BlockSpec((tm, D_MODEL), row),
            pl.BlockSpec((tm, ROUTE_LANES), row),
            pl.BlockSpec((tm, D_MODEL), row),
            pl.BlockSpec((None, 6, D_MODEL), lambda i: (i // tiles_per_mod, 0, 0)),
            pl.BlockSpec((1, D_MODEL), lambda i: (0, 0)),
        ],
        out_specs=pl.BlockSpec((tm, D_MODEL), row),
        out_shape=jax.ShapeDtypeStruct((n, D_MODEL), F32),
        compiler_params=_params("arbitrary"),
        name="moe_combine",
    )(y1, y2, route, x, mods, g)


def _moe(h, x, route, mods, g, w_gu, w_down):
    n = x.shape[0]
    tm = MOE_TILE
    rows = 2 * n + N_EXPERTS * tm
    expert = jnp.concatenate([route[:, 0], route[:, 1]]).astype(jnp.int32)
    onehot = (expert[:, None] == jnp.arange(N_EXPERTS)[None, :]).astype(jnp.int32)
    rank = jnp.take_along_axis(jnp.cumsum(onehot, axis=0), expert[:, None], axis=1)[:, 0] - 1
    counts = jnp.sum(onehot, axis=0)
    padded = ((counts + tm - 1) // tm) * tm
    ends = jnp.cumsum(padded)
    starts = ends - padded
    pos = starts[expert] + rank
    token = jnp.tile(jnp.arange(n, dtype=jnp.int32), 2)
    src = jnp.zeros((rows,), jnp.int32).at[pos].set(token)
    tile_start = jnp.arange(rows // tm, dtype=jnp.int32) * tm
    tile_expert = jnp.minimum(
        jnp.sum((tile_start[:, None] >= ends[None, :]).astype(jnp.int32), axis=1), N_EXPERTS - 1)
    n_active = (ends[-1:] // tm).astype(jnp.int32)
    hs = jnp.take(h, src, axis=0)
    ys = _moe_experts(hs, tile_expert, n_active, w_gu, w_down)
    y1 = jnp.take(ys, pos[:n], axis=0)
    y2 = jnp.take(ys, pos[n:], axis=0)
    return _moe_combine(y1, y2, route, x, mods, g)


def _chunked(w, chunk):
    k, n = w.shape
    return jnp.transpose(w.reshape(k, n // chunk, chunk), (1, 0, 2))


def kernel(x_prompt, x_sample, c, c_ctx, cache_k, cache_v, w_mod, b_mod, g_pre_mix, g_post_mix,
           g_pre_ffn, g_post_ffn, w_in, w_out, attn_sink, conv_w, conv_b, conv_norm_g, conv_norm_b,
           ffn_w_gu, ffn_w_down, moe_w_router, moe_b_router, moe_w_gu, moe_w_down):
    batch, seq, _ = x_prompt.shape
    dec_batch, dec_seq, _ = x_sample.shape
    past = cache_k.shape[2]

    cvec = jnp.zeros((MOD_ROWS, D_MODEL), F32).at[0].set(c_ctx).at[1:1 + dec_batch].set(c)
    mods = _modulations(cvec, w_mod, b_mod)

    w_in_b = w_in.astype(BF16)
    w_out_b = w_out.astype(BF16)
    ffn_gu_b = ffn_w_gu.astype(BF16)
    ffn_down_b = ffn_w_down.astype(BF16)
    moe_gu_b = moe_w_gu.astype(BF16)
    moe_down_b = moe_w_down.astype(BF16)
    cache_k4 = cache_k.reshape(dec_batch, DEPTH, past, D_KV)
    cache_v4 = cache_v.reshape(dec_batch, DEPTH, past, D_KV)
    rope_tabs = _rope_tables(dec_seq)
    gavg = jnp.kron(jnp.eye(CONV_GROUPS, dtype=F32),
                    jnp.full((D_CONV // CONV_GROUPS,) * 2, CONV_GROUPS / D_CONV, F32)).astype(BF16)

    def layer(l, x, mods_l, latent):
        n_seq, seq_len = (dec_batch, dec_seq) if latent else (batch, seq)
        row = lambda v: v[l][None, :]
        q, k, v, z = _inproj(x, mods_l, row(g_pre_mix), w_in_b[l], rope_tabs if latent else None, seq_len)
        if latent:
            a = _attention_latent(q, k, v, cache_k4, cache_v4, attn_sink[l], l, n_seq, seq_len)
        else:
            a = _attention_context(q, k, v, attn_sink[l], n_seq, seq_len)
        cm = _conv_module(z, conv_w[l], row(conv_b), row(conv_norm_g), row(conv_norm_b), gavg, seq_len)
        if l % 2 == 0:
            x1, h = _outproj(a, cm, x, mods_l, row(g_post_mix), row(g_pre_ffn), w_out_b[l], None)
            gu = ffn_gu_b[l // 2]
            x2 = _ffn_dense(h, x1, mods_l, row(g_post_ffn),
                            _chunked(gu[:, :D_FF], FF_CHUNK), _chunked(gu[:, D_FF:], FF_CHUNK),
                            ffn_down_b[l // 2].reshape(N_FF_CHUNKS, FF_CHUNK, D_MODEL))
        else:
            w_r = jnp.zeros((D_MODEL, ROUTE_LANES), F32).at[:, :N_EXPERTS].set(moe_w_router[l // 2])
            b_r = jnp.zeros((1, ROUTE_LANES), F32).at[0, :N_EXPERTS].set(moe_b_router[l // 2])
            x1, h, route = _outproj(a, cm, x, mods_l, row(g_post_mix), row(g_pre_ffn), w_out_b[l],
                                    (w_r, b_r))
            x2 = _moe(h, x1, route, mods_l, row(g_post_ffn), moe_gu_b[l // 2], moe_down_b[l // 2])
        return x2, k, v

    xp = x_prompt.reshape(batch * seq, D_MODEL)
    ks, vs = [], []
    for l in range(DEPTH):
        xp, k_l, v_l = layer(l, xp, mods[l, 0:1], False)
        ks.append(k_l.reshape(batch, seq, N_KV_HEADS, HEAD_DIM))
        vs.append(v_l.reshape(batch, seq, N_KV_HEADS, HEAD_DIM))

    xs = x_sample.reshape(dec_batch * dec_seq, D_MODEL)
    for l in range(DEPTH):
        xs, _, _ = layer(l, xs, mods[l, 1:1 + dec_batch], True)

    return (xp.reshape(batch, seq, D_MODEL), xs.reshape(dec_batch, dec_seq, D_MODEL),
            jnp.stack(ks, axis=1), jnp.stack(vs, axis=1))
```

```python
import functools
import math

import jax
import jax.numpy as jnp
from jax import lax
from jax.experimental import pallas as pl
from jax.experimental.pallas import tpu as pltpu

F32 = jnp.float32
BF16 = jnp.bfloat16

D_MODEL = 1024
DEPTH = 4
GRID_W = 64
D_ATTN = 512
D_CONV = 512
HEAD_DIM = 64
N_HEADS = 8
N_KV_HEADS = 2
N_REP = N_HEADS // N_KV_HEADS
D_KV = N_KV_HEADS * HEAD_DIM
D_IN = D_ATTN + 2 * D_KV + 2 * D_CONV
WINDOW = 128
BLOCK = 128
ATTN_SCALE = HEAD_DIM ** -0.5
ROPE_BASE = 10000.0
CONV_WIDTH = 31
CONV_HALF = CONV_WIDTH // 2
CONV_GROUPS = 8
D_FF = 2816
N_EXPERTS = 8
D_FF_EXPERT = 3584
EPS = 1e-6
NEG_INF = -1e30
LOG2E = math.log2(math.e)

LANES = 128
SUBLANES = 8
VMEM_LIMIT = 48 * 1024 * 1024

TOKEN_TILE = 512
ATTN_BLOCKS = 4
CONV_TILE = 256
CONV_HALO = 16
CONV_ROWS = 128
FF_CHUNK = 256
MOE_TILE = 512
MOE_CHUNK = D_FF_EXPERT // 2
N_MOE_CHUNKS = D_FF_EXPERT // MOE_CHUNK
MOD_ROWS = 16
ROUTE_LANES = 128

assert N_KV_HEADS * HEAD_DIM == LANES and D_ATTN == N_REP * LANES
assert WINDOW == BLOCK


def _params(*sem):
    return pltpu.CompilerParams(dimension_semantics=sem, vmem_limit_bytes=VMEM_LIMIT)


def _silu(x):
    return x * jax.nn.sigmoid(x)


def _rms(x, g):
    return x * lax.rsqrt(jnp.mean(x * x, axis=-1, keepdims=True) + EPS) * g


def _split_bf16(x):
    hi = x.astype(BF16)
    lo = (x - hi.astype(F32)).astype(BF16)
    return hi, lo


def _rows(arr, idx):
    return arr.at[idx].get(mode="promise_in_bounds")


def _mod_kernel(c_ref, w_ref, b_ref, o_ref):
    a = _silu(c_ref[...]).astype(BF16)
    o_ref[...] = jnp.dot(a, w_ref[...].astype(BF16), preferred_element_type=F32) + b_ref[...]


def _modulations(cvec, w_mod, b_mod):
    out = pl.pallas_call(
        _mod_kernel,
        grid=(DEPTH, 6),
        in_specs=[
            pl.BlockSpec((MOD_ROWS, D_MODEL), lambda l, s: (0, 0)),
            pl.BlockSpec((None, D_MODEL, D_MODEL), lambda l, s: (l, 0, s)),
            pl.BlockSpec((None, 1, D_MODEL), lambda l, s: (l, 0, s)),
        ],
        out_specs=pl.BlockSpec((None, MOD_ROWS, D_MODEL), lambda l, s: (l, 0, s)),
        out_shape=jax.ShapeDtypeStruct((DEPTH, MOD_ROWS, 6 * D_MODEL), F32),
        compiler_params=_params("arbitrary", "arbitrary"),
        name="modulations",
    )(cvec, w_mod, b_mod.reshape(DEPTH, 1, 6 * D_MODEL))
    return out.reshape(DEPTH, MOD_ROWS, 6, D_MODEL)


def _rope(x, cos, sin):
    lane = lax.broadcasted_iota(jnp.int32, x.shape, 1)
    width = x.shape[1]
    partner = jnp.where((lane & 16) == 0, pltpu.roll(x, width - 16, 1), pltpu.roll(x, 16, 1))
    return x * cos + partner * sin


def _inproj_kernel(*refs, rope):
    if rope:
        x_ref, mod_ref, g_ref, w_ref, cos_ref, sin_ref, q_ref, k_ref, v_ref, z_ref = refs
    else:
        x_ref, mod_ref, g_ref, w_ref, q_ref, k_ref, v_ref, z_ref = refs
    h = _rms(x_ref[...], g_ref[...]) * (1.0 + mod_ref[1:2, :]) + mod_ref[0:1, :]
    p = jnp.dot(h.astype(BF16), w_ref[...], preferred_element_type=F32)
    q = p[:, :D_ATTN]
    k = p[:, D_ATTN:D_ATTN + D_KV]
    if rope:
        cos = cos_ref[...]
        sin = sin_ref[...]
        q = _rope(q, cos, sin)
        k = _rope(k, cos[:, :D_KV], sin[:, :D_KV])
    q_ref[...] = (q * (ATTN_SCALE * LOG2E)).astype(BF16)
    k_ref[...] = k
    v_ref[...] = p[:, D_ATTN + D_KV:D_ATTN + 2 * D_KV]
    a = p[:, D_ATTN + 2 * D_KV:D_ATTN + 2 * D_KV + D_CONV]
    gt = p[:, D_ATTN + 2 * D_KV + D_CONV:]
    z_ref[...] = (a * jax.nn.sigmoid(gt)).astype(BF16)


def _inproj(x, mods, g, w_in, layer, rope_tabs, seq_len):
    n = x.shape[0]
    tm = TOKEN_TILE
    tiles_per_mod = (n // mods.shape[0]) // tm
    row = lambda i: (i, 0)
    const = lambda i: (0, 0)
    in_specs = [
        pl.BlockSpec((tm, D_MODEL), row),
        pl.BlockSpec((None, 6, D_MODEL), lambda i: (i // tiles_per_mod, 0, 0)),
        pl.BlockSpec((1, D_MODEL), const),
        pl.BlockSpec((None, D_MODEL, D_IN), lambda i: (layer, 0, 0)),
    ]
    args = [x, mods, g, w_in]
    if rope_tabs is not None:
        tiles_per_seq = seq_len // tm
        tab = pl.BlockSpec((tm, D_ATTN), lambda i: (i % tiles_per_seq, 0))
        in_specs += [tab, tab]
        args += list(rope_tabs)
    return pl.pallas_call(
        functools.partial(_inproj_kernel, rope=rope_tabs is not None),
        grid=(n // tm,),
        in_specs=in_specs,
        out_specs=[
            pl.BlockSpec((tm, D_ATTN), row),
            pl.BlockSpec((tm, D_KV), row),
            pl.BlockSpec((tm, D_KV), row),
            pl.BlockSpec((tm, D_CONV), row),
        ],
        out_shape=[
            jax.ShapeDtypeStruct((n, D_ATTN), BF16),
            jax.ShapeDtypeStruct((n, D_KV), F32),
            jax.ShapeDtypeStruct((n, D_KV), F32),
            jax.ShapeDtypeStruct((n, D_CONV), BF16),
        ],
        compiler_params=_params("arbitrary"),
        name="inproj",
    )(*args)


def _rope_tables(seq_len):
    half = HEAD_DIM // 2
    n_freq = half // 2
    t = jnp.arange(seq_len)
    inv = ROPE_BASE ** (-jnp.arange(n_freq, dtype=F32) * 2.0 / half)
    ang_r = (t // GRID_W).astype(F32)[:, None] * inv[None, :]
    ang_c = (t % GRID_W).astype(F32)[:, None] * inv[None, :]
    cos = jnp.concatenate([jnp.cos(ang_r)] * 2 + [jnp.cos(ang_c)] * 2, axis=-1)
    sin = jnp.concatenate([-jnp.sin(ang_r), jnp.sin(ang_r), -jnp.sin(ang_c), jnp.sin(ang_c)], axis=-1)
    return jnp.tile(cos, (1, N_HEADS)), jnp.tile(sin, (1, N_HEADS))


def _pair_heads(w, axis):
    shape = w.shape
    split = shape[:axis] + (N_KV_HEADS, N_REP, HEAD_DIM) + shape[axis + 1:]
    return jnp.swapaxes(w.reshape(split), axis, axis + 1).reshape(shape)


def _attend(q, sink_ref, ks, vs, prev_ok=None, next_ok=None):
    qb = q.shape[0]
    past = ks[0].shape[0]
    windowed = len(ks) > 1
    low = lax.broadcasted_iota(jnp.int32, (qb, LANES), 1) < HEAD_DIM
    zero = jnp.zeros((qb, LANES), BF16)
    blocks, sinks = [], []
    for m in range(N_REP):
        slab = q[:, m * LANES:(m + 1) * LANES]
        blocks += [jnp.where(low, slab, zero), jnp.where(low, zero, slab)]
        sinks += [jnp.full((qb, 1), sink_ref[m], F32), jnp.full((qb, 1), sink_ref[N_REP + m], F32)]
    qz = jnp.concatenate(blocks, axis=0)
    sink = jnp.concatenate(sinks, axis=0)
    k_all = jnp.concatenate(ks, axis=0).astype(BF16)
    v_all = jnp.concatenate(vs, axis=0).astype(BF16)
    v_ext = jnp.concatenate([v_all, jnp.ones_like(v_all)], axis=1)
    s = lax.dot_general(qz, k_all, (((1,), (1,)), ((), ())), preferred_element_type=F32)
    pieces = [s[:, c:c + LANES] for c in range(0, past, LANES)]
    if windowed:
        qi = lax.broadcasted_iota(jnp.int32, (qb, LANES), 0)
        kk = lax.broadcasted_iota(jnp.int32, (qb, LANES), 1)
        bias_prev = jnp.where((kk >= qi) & prev_ok, 0.0, NEG_INF)
        bias_next = jnp.where((kk <= qi) & next_ok, 0.0, NEG_INF)
        rep = lambda b: jnp.concatenate([b] * N_HEADS, axis=0)
        pieces += [s[:, past:past + BLOCK] + rep(bias_prev),
                   s[:, past + BLOCK:past + 2 * BLOCK],
                   s[:, past + 2 * BLOCK:] + rep(bias_next)]
    m = jnp.maximum(jnp.max(functools.reduce(jnp.maximum, pieces), axis=-1, keepdims=True), sink)
    p = jnp.concatenate([jnp.exp2(x - m).astype(BF16) for x in pieces], axis=1)
    o = jnp.dot(p, v_ext, preferred_element_type=F32)
    r = o[:, :LANES] / (o[:, LANES:] + jnp.exp2(sink - m))
    slabs = [jnp.where(low, r[2 * m * qb:(2 * m + 1) * qb], r[(2 * m + 1) * qb:(2 * m + 2) * qb])
             for m in range(N_REP)]
    return jnp.concatenate(slabs, axis=1).astype(BF16)


def _attn_context_kernel(sink_ref, q_ref, k_ref, v_ref, o_ref):
    o_ref[...] = _attend(q_ref[...], sink_ref, [k_ref[...]], [v_ref[...]])


def _attn_latent_kernel(sink_ref, q_ref, ck_ref, cv_ref, kp_ref, kc_ref, kn_ref, vp_ref, vc_ref, vn_ref,
                        o_ref, *, n_blocks):
    j = pl.program_id(1)
    k_blocks = [kp_ref[...]] + [kc_ref[s * BLOCK:(s + 1) * BLOCK, :] for s in range(ATTN_BLOCKS)] + [kn_ref[...]]
    v_blocks = [vp_ref[...]] + [vc_ref[s * BLOCK:(s + 1) * BLOCK, :] for s in range(ATTN_BLOCKS)] + [vn_ref[...]]
    ck = ck_ref[...]
    cv = cv_ref[...]
    for s in range(ATTN_BLOCKS):
        block = j * ATTN_BLOCKS + s
        o_ref[s * BLOCK:(s + 1) * BLOCK, :] = _attend(
            q_ref[s * BLOCK:(s + 1) * BLOCK, :], sink_ref,
            [ck] + k_blocks[s:s + 3], [cv] + v_blocks[s:s + 3],
            prev_ok=block > 0, next_ok=block < n_blocks - 1)


def _attention_latent(q, k, v, cache_k, cache_v, sink, layer, batch, seq_len):
    nb = seq_len // BLOCK
    q3 = q.reshape(batch, seq_len, D_ATTN)
    k3 = k.reshape(batch, seq_len, D_KV)
    v3 = v.reshape(batch, seq_len, D_KV)
    past = cache_k.shape[2]
    blk = lambda f: pl.BlockSpec((None, BLOCK, D_KV), f)
    prev_map = lambda b, j: (b, jnp.maximum(j * ATTN_BLOCKS - 1, 0), 0)
    cur_map = lambda b, j: (b, j, 0)
    next_map = lambda b, j: (b, jnp.minimum((j + 1) * ATTN_BLOCKS, nb - 1), 0)
    cur = pl.BlockSpec((None, ATTN_BLOCKS * BLOCK, D_KV), cur_map)
    cache_spec = pl.BlockSpec((None, None, past, D_KV), lambda b, j: (b, layer, 0, 0))
    out = pl.pallas_call(
        functools.partial(_attn_latent_kernel, n_blocks=nb),
        grid=(batch, nb // ATTN_BLOCKS),
        in_specs=[
            pl.BlockSpec(memory_space=pltpu.SMEM),
            pl.BlockSpec((None, ATTN_BLOCKS * BLOCK, D_ATTN), cur_map),
            cache_spec, cache_spec,
            blk(prev_map), cur, blk(next_map),
            blk(prev_map), cur, blk(next_map),
        ],
        out_specs=pl.BlockSpec((None, ATTN_BLOCKS * BLOCK, D_ATTN), cur_map),
        out_shape=jax.ShapeDtypeStruct((batch, seq_len, D_ATTN), BF16),
        compiler_params=_params("arbitrary", "arbitrary"),
        name="attn_latent",
    )(sink, q3, cache_k, cache_v, k3, k3, k3, v3, v3, v3)
    return out.reshape(batch * seq_len, D_ATTN)


def _attention_context(q, k, v, sink, batch, seq_len):
    q3 = q.reshape(batch, seq_len, D_ATTN)
    k3 = k.reshape(batch, seq_len, D_KV)
    v3 = v.reshape(batch, seq_len, D_KV)
    seq = lambda b: (b, 0, 0)
    out = pl.pallas_call(
        _attn_context_kernel,
        grid=(batch,),
        in_specs=[
            pl.BlockSpec(memory_space=pltpu.SMEM),
            pl.BlockSpec((None, seq_len, D_ATTN), seq),
            pl.BlockSpec((None, seq_len, D_KV), seq),
            pl.BlockSpec((None, seq_len, D_KV), seq),
        ],
        out_specs=pl.BlockSpec((None, seq_len, D_ATTN), seq),
        out_shape=jax.ShapeDtypeStruct((batch, seq_len, D_ATTN), BF16),
        compiler_params=_params("arbitrary"),
        name="attn_context",
    )(sink, q3, k3, v3)
    return out.reshape(batch * seq_len, D_ATTN)


def _conv_kernel(z_ref, zp_ref, zn_ref, w_ref, b_ref, gavg_ref, gn_ref, bn_ref, o_ref,
                 ext_ref, sh_ref, y_ref, *, tiles_per_seq):
    i = pl.program_id(0)
    tt = z_ref.shape[0]
    first = (i % tiles_per_seq) == 0
    last = (i % tiles_per_seq) == tiles_per_seq - 1
    ext_ref[0:CONV_HALO, :] = jnp.where(first, 0.0, zp_ref[...].astype(F32))
    ext_ref[CONV_HALO:CONV_HALO + tt, :] = z_ref[...].astype(F32)
    ext_ref[CONV_HALO + tt:, :] = jnp.where(last, 0.0, zn_ref[...].astype(F32))
    base = CONV_HALO - CONV_HALF
    span = sh_ref.shape[1]
    for r in range(SUBLANES):
        sh_ref[r] = ext_ref[base + r:base + r + span, :]
    for c in range(D_CONV // LANES):
        cs = slice(c * LANES, (c + 1) * LANES)
        w = w_ref[:, cs]
        for rb in range(tt // CONV_ROWS):
            r0 = rb * CONV_ROWS
            acc = jnp.broadcast_to(b_ref[:, cs], (CONV_ROWS, LANES))
            for j in range(CONV_WIDTH):
                r, off = j % SUBLANES, r0 + SUBLANES * (j // SUBLANES)
                acc = acc + w[j:j + 1, :] * sh_ref[r, off:off + CONV_ROWS, cs]
            y_ref[r0:r0 + CONV_ROWS, cs] = acc
    y = y_ref[...]
    gavg = gavg_ref[...]

    def group_mean(a):
        hi, lo = _split_bf16(a)
        return (jnp.dot(hi, gavg, preferred_element_type=F32)
                + jnp.dot(lo, gavg, preferred_element_type=F32))

    d = y - group_mean(y)
    var = group_mean(d * d)
    zn = d * lax.rsqrt(var + EPS) * gn_ref[...] + bn_ref[...]
    o_ref[...] = _silu(zn).astype(BF16)


def _conv_module(z, conv_w, conv_b, gn, bn, gavg, layer, seq_len):
    n = z.shape[0]
    tt = CONV_TILE
    halo_per_tile = tt // CONV_HALO
    n_halo = n // CONV_HALO
    const = lambda i: (0, 0)
    per_layer = lambda i: (layer, 0, 0)
    span = tt + SUBLANES * ((CONV_WIDTH - 1) // SUBLANES)
    return pl.pallas_call(
        functools.partial(_conv_kernel, tiles_per_seq=seq_len // tt),
        grid=(n // tt,),
        in_specs=[
            pl.BlockSpec((tt, D_CONV), lambda i: (i, 0)),
            pl.BlockSpec((CONV_HALO, D_CONV), lambda i: (jnp.maximum(i * halo_per_tile - 1, 0), 0)),
            pl.BlockSpec((CONV_HALO, D_CONV),
                         lambda i: (jnp.minimum((i + 1) * halo_per_tile, n_halo - 1), 0)),
            pl.BlockSpec((None, CONV_WIDTH, D_CONV), per_layer),
            pl.BlockSpec((None, 1, D_CONV), per_layer),
            pl.BlockSpec((D_CONV, D_CONV), const),
            pl.BlockSpec((None, 1, D_CONV), per_layer),
            pl.BlockSpec((None, 1, D_CONV), per_layer),
        ],
        out_specs=pl.BlockSpec((tt, D_CONV), lambda i: (i, 0)),
        out_shape=jax.ShapeDtypeStruct((n, D_CONV), BF16),
        scratch_shapes=[
            pltpu.VMEM((tt + 2 * CONV_HALO, D_CONV), F32),
            pltpu.VMEM((SUBLANES, span, D_CONV), F32),
            pltpu.VMEM((tt, D_CONV), F32),
        ],
        compiler_params=_params("arbitrary"),
        name="conv_module",
    )(z, z, z, conv_w, conv_b, gavg, gn, bn)


def _outproj_kernel(*refs, routed):
    if routed:
        (a_ref, cm_ref, x_ref, mod_ref, gpost_ref, gpre_ref, w_ref, wr_ref, br_ref, tri_ref,
         x1_ref, h_ref, route_ref, count_ref, carry_ref) = refs
    else:
        a_ref, cm_ref, x_ref, mod_ref, gpost_ref, gpre_ref, w_ref, x1_ref, h_ref = refs
    m = (jnp.dot(a_ref[...], w_ref[:D_ATTN, :], preferred_element_type=F32)
         + jnp.dot(cm_ref[...], w_ref[D_ATTN:, :], preferred_element_type=F32))
    x1 = x_ref[...] + mod_ref[2:3, :] * _rms(m, gpost_ref[...])
    x1_ref[...] = x1
    h = _rms(x1, gpre_ref[...]) * (1.0 + mod_ref[4:5, :]) + mod_ref[3:4, :]
    h_ref[...] = h.astype(BF16)
    if routed:
        h_hi, h_lo = _split_bf16(h)
        w_hi, w_lo = _split_bf16(wr_ref[...])
        logits = (jnp.dot(h_hi, w_hi, preferred_element_type=F32)
                  + jnp.dot(h_hi, w_lo, preferred_element_type=F32)
                  + jnp.dot(h_lo, w_hi, preferred_element_type=F32)) + br_ref[...]
        lane = lax.broadcasted_iota(jnp.int32, logits.shape, 1).astype(F32)
        logits = jnp.where(lane < N_EXPERTS, logits, -jnp.inf)
        m1 = jnp.max(logits, axis=-1, keepdims=True)
        i1 = jnp.min(jnp.where(logits == m1, lane, float(ROUTE_LANES)), axis=-1, keepdims=True)
        rest = jnp.where(lane == i1, -jnp.inf, logits)
        m2 = jnp.max(rest, axis=-1, keepdims=True)
        i2 = jnp.min(jnp.where(rest == m2, lane, float(ROUTE_LANES)), axis=-1, keepdims=True)
        e2 = jnp.exp(m2 - m1)
        g1 = 1.0 / (1.0 + e2)
        g2 = e2 * g1

        @pl.when(pl.program_id(0) == 0)
        def _():
            carry_ref[...] = jnp.zeros_like(carry_ref)

        hit1 = lane == i1
        hit2 = lane == i2
        picked = jnp.where(hit1, 1.0, jnp.where(hit2, 1.0, 0.0))
        before = carry_ref[...] + jnp.dot(tri_ref[...], picked.astype(BF16), preferred_element_type=F32)
        rank1 = jnp.sum(jnp.where(hit1, before, 0.0), axis=-1, keepdims=True)
        rank2 = jnp.sum(jnp.where(hit2, before, 0.0), axis=-1, keepdims=True)
        total = carry_ref[...] + jnp.sum(picked, axis=0, keepdims=True)
        carry_ref[...] = total
        count_ref[...] = total
        cols = (i1, i2, g1, g2, rank1, rank2)
        route = jnp.zeros_like(logits)
        for c, val in enumerate(cols):
            route = jnp.where(lane == c, val, route)
        route_ref[...] = route


def _outproj(a, cm, x, mods, gpost, gpre, w_out, layer, router):
    n = x.shape[0]
    tm = TOKEN_TILE
    tiles_per_mod = (n // mods.shape[0]) // tm
    row = lambda i: (i, 0)
    const = lambda i: (0, 0)
    in_specs = [
        pl.BlockSpec((tm, D_ATTN), row),
        pl.BlockSpec((tm, D_CONV), row),
        pl.BlockSpec((tm, D_MODEL), row),
        pl.BlockSpec((None, 6, D_MODEL), lambda i: (i // tiles_per_mod, 0, 0)),
        pl.BlockSpec((1, D_MODEL), const),
        pl.BlockSpec((1, D_MODEL), const),
        pl.BlockSpec((None, D_MODEL, D_MODEL), lambda i: (layer, 0, 0)),
    ]
    args = [a, cm, x, mods, gpost, gpre, w_out]
    out_specs = [pl.BlockSpec((tm, D_MODEL), row), pl.BlockSpec((tm, D_MODEL), row)]
    out_shape = [jax.ShapeDtypeStruct((n, D_MODEL), F32), jax.ShapeDtypeStruct((n, D_MODEL), BF16)]
    scratch = []
    if router is not None:
        in_specs += [pl.BlockSpec((D_MODEL, ROUTE_LANES), const), pl.BlockSpec((1, ROUTE_LANES), const),
                     pl.BlockSpec((tm, tm), const)]
        args += list(router)
        out_specs += [pl.BlockSpec((tm, ROUTE_LANES), row), pl.BlockSpec((1, ROUTE_LANES), const)]
        out_shape += [jax.ShapeDtypeStruct((n, ROUTE_LANES), F32),
                      jax.ShapeDtypeStruct((1, ROUTE_LANES), F32)]
        scratch = [pltpu.VMEM((1, ROUTE_LANES), F32)]
    return pl.pallas_call(
        functools.partial(_outproj_kernel, routed=router is not None),
        grid=(n // tm,),
        in_specs=in_specs,
        out_specs=out_specs,
        out_shape=out_shape,
        scratch_shapes=scratch,
        compiler_params=_params("arbitrary"),
        name="outproj",
    )(*args)


def _swiglu_hidden(h, wgu_ref, act_ref, gate_col, up_col, width):
    for c in range(0, width, FF_CHUNK):
        g = jnp.dot(h, wgu_ref[:, gate_col + c:gate_col + c + FF_CHUNK], preferred_element_type=F32)
        u = jnp.dot(h, wgu_ref[:, up_col + c:up_col + c + FF_CHUNK], preferred_element_type=F32)
        act_ref[:, c:c + FF_CHUNK] = (_silu(g) * u).astype(BF16)


def _ffn_kernel(h_ref, x_ref, mod_ref, g_ref, wgu_ref, wd_ref, o_ref, act_ref):
    _swiglu_hidden(h_ref[...], wgu_ref, act_ref, 0, D_FF, D_FF)
    f = jnp.dot(act_ref[...], wd_ref[...], preferred_element_type=F32)
    o_ref[...] = x_ref[...] + mod_ref[5:6, :] * _rms(f, g_ref[...])


def _ffn_dense(h, x, mods, g, w_gu, w_down, index):
    n = x.shape[0]
    tm = TOKEN_TILE
    tiles_per_mod = (n // mods.shape[0]) // tm
    row = lambda i: (i, 0)
    whole = lambda i: (index, 0, 0)
    resident = pl.Buffered(1)
    return pl.pallas_call(
        _ffn_kernel,
        grid=(n // tm,),
        in_specs=[
            pl.BlockSpec((tm, D_MODEL), row),
            pl.BlockSpec((tm, D_MODEL), row),
            pl.BlockSpec((None, 6, D_MODEL), lambda i: (i // tiles_per_mod, 0, 0)),
            pl.BlockSpec((1, D_MODEL), lambda i: (0, 0)),
            pl.BlockSpec((None, D_MODEL, 2 * D_FF), whole, pipeline_mode=resident),
            pl.BlockSpec((None, D_FF, D_MODEL), whole, pipeline_mode=resident),
        ],
        out_specs=pl.BlockSpec((tm, D_MODEL), row),
        out_shape=jax.ShapeDtypeStruct((n, D_MODEL), F32),
        scratch_shapes=[pltpu.VMEM((tm, D_FF), BF16)],
        compiler_params=_params("arbitrary"),
        name="ffn_dense",
    )(h, x, mods, g, w_gu, w_down)


def _moe_kernel(te_ref, na_ref, h_ref, wg_ref, wu_ref, wd_ref, o_ref, act_ref):
    i = pl.program_id(0)
    j = pl.program_id(1)

    @pl.when(i < na_ref[0])
    def _():
        h = h_ref[...]
        for c in range(0, MOE_CHUNK, FF_CHUNK):
            g = jnp.dot(h, wg_ref[:, c:c + FF_CHUNK], preferred_element_type=F32)
            u = jnp.dot(h, wu_ref[:, c:c + FF_CHUNK], preferred_element_type=F32)
            act_ref[:, c:c + FF_CHUNK] = (_silu(g) * u).astype(BF16)
        part = jnp.dot(act_ref[...], wd_ref[...], preferred_element_type=F32)

        @pl.when(j == 0)
        def _():
            o_ref[...] = part

        @pl.when(j > 0)
        def _():
            o_ref[...] += part


def _moe_experts(hs, tile_expert, n_active, w_gu, w_down, index):
    rows = hs.shape[0]
    tm = MOE_TILE

    def tile(i, na):
        return jnp.minimum(i, na[0] - 1)

    def chunk(i, j, na):
        return jnp.where(i < na[0], j, N_MOE_CHUNKS - 1)

    grid_spec = pltpu.PrefetchScalarGridSpec(
        num_scalar_prefetch=2,
        grid=(rows // tm, N_MOE_CHUNKS),
        in_specs=[
            pl.BlockSpec((tm, D_MODEL), lambda i, j, te, na: (tile(i, na), 0)),
            pl.BlockSpec((None, None, D_MODEL, MOE_CHUNK),
                         lambda i, j, te, na: (index, te[tile(i, na)], 0, chunk(i, j, na))),
            pl.BlockSpec((None, None, D_MODEL, MOE_CHUNK),
                         lambda i, j, te, na: (index, te[tile(i, na)], 0, N_MOE_CHUNKS + chunk(i, j, na))),
            pl.BlockSpec((None, None, MOE_CHUNK, D_MODEL),
                         lambda i, j, te, na: (index, te[tile(i, na)], chunk(i, j, na), 0)),
        ],
        out_specs=pl.BlockSpec((tm, D_MODEL), lambda i, j, te, na: (tile(i, na), 0)),
        scratch_shapes=[pltpu.VMEM((tm, MOE_CHUNK), BF16)],
    )
    return pl.pallas_call(
        _moe_kernel,
        grid_spec=grid_spec,
        out_shape=jax.ShapeDtypeStruct((rows, D_MODEL), F32),
        compiler_params=_params("arbitrary", "arbitrary"),
        name="moe_experts",
    )(tile_expert, n_active, hs, w_gu, w_gu, w_down)


def _combine_kernel(y1_ref, y2_ref, route_ref, x_ref, mod_ref, g_ref, o_ref):
    f = route_ref[:, 2:3] * y1_ref[...] + route_ref[:, 3:4] * y2_ref[...]
    o_ref[...] = x_ref[...] + mod_ref[5:6, :] * _rms(f, g_ref[...])


def _moe_combine(y1, y2, route, x, mods, g):
    n = x.shape[0]
    tm = TOKEN_TILE
    tiles_per_mod = (n // mods.shape[0]) // tm
    row = lambda i: (i, 0)
    return pl.pallas_call(
        _combine_kernel,
        grid=(n // tm,),
        in_specs=[
            pl.BlockSpec((tm, D_MODEL), row),
            pl.BlockSpec((tm, D_MODEL), row),
            pl.BlockSpec((tm, ROUTE_LANES), row),
            pl.BlockSpec((tm, D_MODEL), row),
            pl.BlockSpec((None, 6, D_MODEL), lambda i: (i // tiles_per_mod, 0, 0)),
            pl.BlockSpec((1, D_MODEL), lambda i: (0, 0)),
        ],
        out_specs=pl.BlockSpec((tm, D_MODEL), row),
        out_shape=jax.ShapeDtypeStruct((n, D_MODEL), F32),
        compiler_params=_params("arbitrary"),
        name="moe_combine",
    )(y1, y2, route, x, mods, g)


def _moe(h, x, route, counts, mods, g, w_gu, w_down, index):
    n = x.shape[0]
    tm = MOE_TILE
    rows = 2 * n + N_EXPERTS * tm
    counts = counts[0, :N_EXPERTS].astype(jnp.int32)
    padded = ((counts + tm - 1) // tm) * tm
    ends = jnp.cumsum(padded)
    starts = ends - padded
    sel = route[:, 0:6].astype(jnp.int32)
    pos1 = _rows(starts, sel[:, 0]) + sel[:, 4]
    pos2 = _rows(starts, sel[:, 1]) + sel[:, 5]
    token = jnp.arange(n, dtype=jnp.int32)
    src = jnp.zeros((rows,), jnp.int32).at[jnp.concatenate([pos1, pos2])].set(
        jnp.concatenate([token, token]), mode="promise_in_bounds", unique_indices=True)
    tile_start = jnp.arange(rows // tm, dtype=jnp.int32) * tm
    tile_expert = jnp.minimum(
        jnp.sum((tile_start[:, None] >= ends[None, :]).astype(jnp.int32), axis=1), N_EXPERTS - 1)
    n_active = (ends[-1:] // tm).astype(jnp.int32)
    ys = _moe_experts(_rows(h, src), tile_expert, n_active, w_gu, w_down, index)
    return _moe_combine(_rows(ys, pos1), _rows(ys, pos2), route, x, mods, g)


def kernel(x_prompt, x_sample, c, c_ctx, cache_k, cache_v, w_mod, b_mod, g_pre_mix, g_post_mix,
           g_pre_ffn, g_post_ffn, w_in, w_out, attn_sink, conv_w, conv_b, conv_norm_g, conv_norm_b,
           ffn_w_gu, ffn_w_down, moe_w_router, moe_b_router, moe_w_gu, moe_w_down):
    batch, seq, _ = x_prompt.shape
    dec_batch, dec_seq, _ = x_sample.shape
    past = cache_k.shape[2]

    cvec = jnp.zeros((MOD_ROWS, D_MODEL), F32).at[0].set(c_ctx).at[1:1 + dec_batch].set(c)
    mods = _modulations(cvec, w_mod, b_mod)

    w_in_b = jnp.concatenate([_pair_heads(w_in[:, :, :D_ATTN], 2), w_in[:, :, D_ATTN:]], axis=2).astype(BF16)
    w_out_b = jnp.concatenate([_pair_heads(w_out[:, :D_ATTN, :], 1), w_out[:, D_ATTN:, :]], axis=1).astype(BF16)
    ffn_gu_b = ffn_w_gu.astype(BF16)
    ffn_down_b = ffn_w_down.astype(BF16)
    moe_gu_b = moe_w_gu.astype(BF16)
    moe_down_b = moe_w_down.astype(BF16)
    w_router = jnp.zeros((DEPTH // 2, D_MODEL, ROUTE_LANES), F32).at[:, :, :N_EXPERTS].set(moe_w_router)
    b_router = jnp.zeros((DEPTH // 2, 1, ROUTE_LANES), F32).at[:, 0, :N_EXPERTS].set(moe_b_router)
    sink2 = attn_sink * LOG2E
    cache_k4 = cache_k.reshape(dec_batch, DEPTH, past, D_KV)
    cache_v4 = cache_v.reshape(dec_batch, DEPTH, past, D_KV)
    rope_tabs = _rope_tables(dec_seq)
    gavg = jnp.kron(jnp.eye(CONV_GROUPS, dtype=F32),
                    jnp.full((D_CONV // CONV_GROUPS,) * 2, CONV_GROUPS / D_CONV, F32)).astype(BF16)
    tri = jnp.tril(jnp.ones((TOKEN_TILE, TOKEN_TILE), F32), -1).astype(BF16)
    conv_b3 = conv_b[:, None, :]
    conv_g3 = conv_norm_g[:, None, :]
    conv_nb3 = conv_norm_b[:, None, :]

    def layer(l, x, mods_l, latent):
        n_seq, seq_len = (dec_batch, dec_seq) if latent else (batch, seq)
        row = lambda v: v[l][None, :]
        q, k, v, z = _inproj(x, mods_l, row(g_pre_mix), w_in_b, l, rope_tabs if latent else None, seq_len)
        if latent:
            a = _attention_latent(q, k, v, cache_k4, cache_v4, sink2[l], l, n_seq, seq_len)
        else:
            a = _attention_context(q, k, v, sink2[l], n_seq, seq_len)
        cm = _conv_module(z, conv_w, conv_b3, conv_g3, conv_nb3, gavg, l, seq_len)
        if l % 2 == 0:
            x1, h = _outproj(a, cm, x, mods_l, row(g_post_mix), row(g_pre_ffn), w_out_b, l, None)
            x2 = _ffn_dense(h, x1, mods_l, row(g_post_ffn), ffn_gu_b, ffn_down_b, l // 2)
        else:
            x1, h, route, counts = _outproj(a, cm, x, mods_l, row(g_post_mix), row(g_pre_ffn), w_out_b, l,
                                            (w_router[l // 2], b_router[l // 2], tri))
            x2 = _moe(h, x1, route, counts, mods_l, row(g_post_ffn), moe_gu_b, moe_down_b, l // 2)
        return x2, k, v

    xp = x_prompt.reshape(batch * seq, D_MODEL)
    ks, vs = [], []
    for l in range(DEPTH):
        xp, k_l, v_l = layer(l, xp, mods[l, 0:1], False)
        ks.append(k_l.reshape(batch, seq, N_KV_HEADS, HEAD_DIM))
        vs.append(v_l.reshape(batch, seq, N_KV_HEADS, HEAD_DIM))

    xs = x_sample.reshape(dec_batch * dec_seq, D_MODEL)
    for l in range(DEPTH):
        xs, _, _ = layer(l, xs, mods[l, 1:1 + dec_batch], True)

    return (xp.reshape(batch, seq, D_MODEL), xs.reshape(dec_batch, dec_seq, D_MODEL),
            jnp.stack(ks, axis=1), jnp.stack(vs, axis=1))
```

```python
import functools
import math

import jax
import jax.numpy as jnp
from jax import lax
from jax.experimental import pallas as pl
from jax.experimental.pallas import tpu as pltpu

F32 = jnp.float32
BF16 = jnp.bfloat16

D_MODEL = 1024
DEPTH = 4
GRID_W = 64
D_ATTN = 512
D_CONV = 512
HEAD_DIM = 64
N_HEADS = 8
N_KV_HEADS = 2
N_REP = N_HEADS // N_KV_HEADS
D_KV = N_KV_HEADS * HEAD_DIM
D_IN = D_ATTN + 2 * D_KV + 2 * D_CONV
WINDOW = 128
BLOCK = 128
ATTN_SCALE = HEAD_DIM ** -0.5
ROPE_BASE = 10000.0
CONV_WIDTH = 31
CONV_HALF = CONV_WIDTH // 2
CONV_GROUPS = 8
D_FF = 2816
N_EXPERTS = 8
D_FF_EXPERT = 3584
EPS = 1e-6
NEG_INF = -1e30
LOG2E = math.log2(math.e)

LANES = 128
SUBLANES = 8
VMEM_LIMIT = 48 * 1024 * 1024

TOKEN_TILE = 1024
ROW_PIECE = 512
FFN_TILE = 512
ATTN_BLOCKS = 4
CONV_HALO = 16
CONV_ROWS = 128
CONV_SPAN = SUBLANES * ((CONV_WIDTH - 1) // SUBLANES)
FF_CHUNK = 256
MOE_TILE = 512
MOE_CHUNK = D_FF_EXPERT // 2
N_MOE_CHUNKS = D_FF_EXPERT // MOE_CHUNK
MOD_ROWS = 16
ROUTE_LANES = 128
CAST_BLOCK_BYTES = 8 * 1024 * 1024

assert N_KV_HEADS * HEAD_DIM == LANES and D_ATTN == N_REP * LANES
assert WINDOW == BLOCK


def _params(*sem):
    return pltpu.CompilerParams(dimension_semantics=sem, vmem_limit_bytes=VMEM_LIMIT)


def _silu(x):
    return x * jax.nn.sigmoid(x)


def _rms(x, g):
    return x * lax.rsqrt(jnp.mean(x * x, axis=-1, keepdims=True) + EPS) * g


def _split_bf16(x):
    hi = x.astype(BF16)
    lo = (x - hi.astype(F32)).astype(BF16)
    return hi, lo


def _rows(arr, idx):
    return arr.at[idx].get(mode="promise_in_bounds")


def _cast_kernel(x_ref, o_ref):
    o_ref[...] = x_ref[...].astype(BF16)


def _to_bf16(w):
    shape = w.shape
    cols = shape[-1]
    w2 = w.reshape(-1, cols)
    rows = w2.shape[0]
    block = 1 << ((CAST_BLOCK_BYTES // (4 * cols)).bit_length() - 1)
    assert rows % block == 0
    out = pl.pallas_call(
        _cast_kernel,
        grid=(rows // block,),
        in_specs=[pl.BlockSpec((block, cols), lambda i: (i, 0))],
        out_specs=pl.BlockSpec((block, cols), lambda i: (i, 0)),
        out_shape=jax.ShapeDtypeStruct((rows, cols), BF16),
        compiler_params=_params("arbitrary"),
        name="weight_cast",
    )(w2)
    return out.reshape(shape)


def _mod_kernel(c_ref, w_ref, b_ref, o_ref):
    a = _silu(c_ref[...]).astype(BF16)
    o_ref[...] = jnp.dot(a, w_ref[...].astype(BF16), preferred_element_type=F32) + b_ref[...]


def _modulations(cvec, w_mod, b_mod):
    out = pl.pallas_call(
        _mod_kernel,
        grid=(DEPTH, 6),
        in_specs=[
            pl.BlockSpec((MOD_ROWS, D_MODEL), lambda l, s: (0, 0)),
            pl.BlockSpec((None, D_MODEL, D_MODEL), lambda l, s: (l, 0, s)),
            pl.BlockSpec((None, 1, D_MODEL), lambda l, s: (l, 0, s)),
        ],
        out_specs=pl.BlockSpec((None, MOD_ROWS, D_MODEL), lambda l, s: (l, 0, s)),
        out_shape=jax.ShapeDtypeStruct((DEPTH, MOD_ROWS, 6 * D_MODEL), F32),
        compiler_params=_params("arbitrary", "arbitrary"),
        name="modulations",
    )(cvec, w_mod, b_mod.reshape(DEPTH, 1, 6 * D_MODEL))
    return out.reshape(DEPTH, MOD_ROWS, 6, D_MODEL)


def _rope(x, cos, sin):
    lane = lax.broadcasted_iota(jnp.int32, x.shape, 1)
    width = x.shape[1]
    partner = jnp.where((lane & 16) == 0, pltpu.roll(x, width - 16, 1), pltpu.roll(x, 16, 1))
    return x * cos + partner * sin


def _inproj_kernel(*refs, rope):
    if rope:
        x_ref, mod_ref, g_ref, w_ref, cos_ref, sin_ref, q_ref, k_ref, v_ref, z_ref = refs
    else:
        x_ref, mod_ref, g_ref, w_ref, q_ref, k_ref, v_ref, z_ref = refs
    for r0 in range(0, x_ref.shape[0], ROW_PIECE):
        rs = slice(r0, r0 + ROW_PIECE)
        h = _rms(x_ref[rs, :], g_ref[...]) * (1.0 + mod_ref[1:2, :]) + mod_ref[0:1, :]
        p = jnp.dot(h.astype(BF16), w_ref[...], preferred_element_type=F32)
        q = p[:, :D_ATTN]
        k = p[:, D_ATTN:D_ATTN + D_KV]
        if rope:
            cos = cos_ref[rs, :]
            sin = sin_ref[rs, :]
            q = _rope(q, jnp.concatenate([cos] * N_REP, axis=1), jnp.concatenate([sin] * N_REP, axis=1))
            k = _rope(k, cos, sin)
        q_ref[rs, :] = (q * (ATTN_SCALE * LOG2E)).astype(BF16)
        k_ref[rs, :] = k
        v_ref[rs, :] = p[:, D_ATTN + D_KV:D_ATTN + 2 * D_KV]
        a = p[:, D_ATTN + 2 * D_KV:D_ATTN + 2 * D_KV + D_CONV]
        gt = p[:, D_ATTN + 2 * D_KV + D_CONV:]
        z_ref[rs, :] = (a * jax.nn.sigmoid(gt)).astype(BF16)


def _inproj(x, mods, g, w_in, layer, rope_tabs, seq_len):
    n = x.shape[0]
    tm = TOKEN_TILE
    tiles_per_mod = (n // mods.shape[0]) // tm
    row = lambda i: (i, 0)
    const = lambda i: (0, 0)
    in_specs = [
        pl.BlockSpec((tm, D_MODEL), row),
        pl.BlockSpec((None, 6, D_MODEL), lambda i: (i // tiles_per_mod, 0, 0)),
        pl.BlockSpec((1, D_MODEL), const),
        pl.BlockSpec((None, D_MODEL, D_IN), lambda i: (layer, 0, 0)),
    ]
    args = [x, mods, g, w_in]
    if rope_tabs is not None:
        tiles_per_seq = seq_len // tm
        tab = pl.BlockSpec((tm, LANES), lambda i: (i % tiles_per_seq, 0))
        in_specs += [tab, tab]
        args += list(rope_tabs)
    return pl.pallas_call(
        functools.partial(_inproj_kernel, rope=rope_tabs is not None),
        grid=(n // tm,),
        in_specs=in_specs,
        out_specs=[
            pl.BlockSpec((tm, D_ATTN), row),
            pl.BlockSpec((tm, D_KV), row),
            pl.BlockSpec((tm, D_KV), row),
            pl.BlockSpec((tm, D_CONV), row),
        ],
        out_shape=[
            jax.ShapeDtypeStruct((n, D_ATTN), BF16),
            jax.ShapeDtypeStruct((n, D_KV), F32),
            jax.ShapeDtypeStruct((n, D_KV), F32),
            jax.ShapeDtypeStruct((n, D_CONV), BF16),
        ],
        compiler_params=_params("arbitrary"),
        name="inproj",
    )(*args)


def _rope_tables(seq_len):
    half = HEAD_DIM // 2
    n_freq = half // 2
    t = jnp.arange(seq_len)
    inv = ROPE_BASE ** (-jnp.arange(n_freq, dtype=F32) * 2.0 / half)
    ang_r = (t // GRID_W).astype(F32)[:, None] * inv[None, :]
    ang_c = (t % GRID_W).astype(F32)[:, None] * inv[None, :]
    cos = jnp.concatenate([jnp.cos(ang_r)] * 2 + [jnp.cos(ang_c)] * 2, axis=-1)
    sin = jnp.concatenate([-jnp.sin(ang_r), jnp.sin(ang_r), -jnp.sin(ang_c), jnp.sin(ang_c)], axis=-1)
    return jnp.tile(cos, (1, LANES // HEAD_DIM)), jnp.tile(sin, (1, LANES // HEAD_DIM))


def _pair_heads(w, axis):
    shape = w.shape
    split = shape[:axis] + (N_KV_HEADS, N_REP, HEAD_DIM) + shape[axis + 1:]
    return jnp.swapaxes(w.reshape(split), axis, axis + 1).reshape(shape)


def _attend(q, sink_ref, ks, vs, prev_ok=None, next_ok=None):
    qb = q.shape[0]
    past = ks[0].shape[0]
    windowed = len(ks) > 1
    low = lax.broadcasted_iota(jnp.int32, (qb, LANES), 1) < HEAD_DIM
    zero = jnp.zeros((qb, LANES), BF16)
    blocks, sinks = [], []
    for g in range(N_REP):
        slab = q[:, g * LANES:(g + 1) * LANES]
        blocks += [jnp.where(low, slab, zero), jnp.where(low, zero, slab)]
        sinks += [jnp.full((qb, 1), sink_ref[g], F32), jnp.full((qb, 1), sink_ref[N_REP + g], F32)]
    qz = jnp.concatenate(blocks, axis=0)
    sink = jnp.concatenate(sinks, axis=0)
    k_all = jnp.concatenate(ks, axis=0).astype(BF16)
    v_all = jnp.concatenate(vs, axis=0).astype(BF16)
    v_ext = jnp.concatenate([v_all, jnp.ones_like(v_all)], axis=1)
    s = lax.dot_general(qz, k_all, (((1,), (1,)), ((), ())), preferred_element_type=F32)
    pieces = [s[:, c:c + LANES] for c in range(0, past, LANES)]
    if windowed:
        qi = lax.broadcasted_iota(jnp.int32, (qb, LANES), 0)
        kk = lax.broadcasted_iota(jnp.int32, (qb, LANES), 1)
        bias_prev = jnp.where((kk >= qi) & prev_ok, 0.0, NEG_INF)
        bias_next = jnp.where((kk <= qi) & next_ok, 0.0, NEG_INF)
        rep = lambda b: jnp.concatenate([b] * N_HEADS, axis=0)
        pieces += [s[:, past:past + BLOCK] + rep(bias_prev),
                   s[:, past + BLOCK:past + 2 * BLOCK],
                   s[:, past + 2 * BLOCK:] + rep(bias_next)]
    m = jnp.maximum(jnp.max(functools.reduce(jnp.maximum, pieces), axis=-1, keepdims=True), sink)
    p = jnp.concatenate([jnp.exp2(x - m).astype(BF16) for x in pieces], axis=1)
    o = jnp.dot(p, v_ext, preferred_element_type=F32)
    r = o[:, :LANES] / (o[:, LANES:] + jnp.exp2(sink - m))
    slabs = [jnp.where(low, r[2 * g * qb:(2 * g + 1) * qb], r[(2 * g + 1) * qb:(2 * g + 2) * qb])
             for g in range(N_REP)]
    return jnp.concatenate(slabs, axis=1).astype(BF16)


def _conv(z, halo_prev, halo_next, w_ref, b_ref, gavg_ref, gn_ref, bn_ref, ext_ref, sh_ref, y_ref):
    tt = z.shape[0]
    ext_ref[0:CONV_HALO, :] = halo_prev
    ext_ref[CONV_HALO:CONV_HALO + tt, :] = z
    ext_ref[CONV_HALO + tt:, :] = halo_next
    base = CONV_HALO - CONV_HALF
    for r in range(SUBLANES):
        sh_ref[r] = ext_ref[base + r:base + r + tt + CONV_SPAN, :]
    for c in range(D_CONV // LANES):
        cs = slice(c * LANES, (c + 1) * LANES)
        w = w_ref[:, cs]
        for rb in range(tt // CONV_ROWS):
            r0 = rb * CONV_ROWS
            acc = jnp.broadcast_to(b_ref[:, cs], (CONV_ROWS, LANES))
            for j in range(CONV_WIDTH):
                r, off = j % SUBLANES, r0 + SUBLANES * (j // SUBLANES)
                acc = acc + w[j:j + 1, :] * sh_ref[r, off:off + CONV_ROWS, cs]
            y_ref[r0:r0 + CONV_ROWS, cs] = acc
    y = y_ref[...]
    gavg = gavg_ref[...]

    def group_mean(a):
        hi, lo = _split_bf16(a)
        return (jnp.dot(hi, gavg, preferred_element_type=F32)
                + jnp.dot(lo, gavg, preferred_element_type=F32))

    d = y - group_mean(y)
    var = group_mean(d * d)
    zn = d * lax.rsqrt(var + EPS) * gn_ref[...] + bn_ref[...]
    return _silu(zn).astype(BF16)


def _mixer_context_kernel(sink_ref, q_ref, k_ref, v_ref, z_ref, w_ref, b_ref, gavg_ref, gn_ref, bn_ref,
                          a_ref, cm_ref, ext_ref, sh_ref, y_ref):
    a_ref[...] = _attend(q_ref[...], sink_ref, [k_ref[...]], [v_ref[...]])
    edge = jnp.zeros((CONV_HALO, D_CONV), F32)
    cm_ref[...] = _conv(z_ref[...].astype(F32), edge, edge, w_ref, b_ref, gavg_ref, gn_ref, bn_ref,
                        ext_ref, sh_ref, y_ref)


def _mixer_latent_kernel(sink_ref, q_ref, ck_ref, cv_ref, kp_ref, kc_ref, kn_ref, vp_ref, vc_ref, vn_ref,
                         z_ref, zp_ref, zn_ref, w_ref, b_ref, gavg_ref, gn_ref, bn_ref,
                         a_ref, cm_ref, ext_ref, sh_ref, y_ref, *, n_blocks):
    j = pl.program_id(1)
    k_blocks = [kp_ref[...]] + [kc_ref[s * BLOCK:(s + 1) * BLOCK, :] for s in range(ATTN_BLOCKS)] + [kn_ref[...]]
    v_blocks = [vp_ref[...]] + [vc_ref[s * BLOCK:(s + 1) * BLOCK, :] for s in range(ATTN_BLOCKS)] + [vn_ref[...]]
    ck = ck_ref[...]
    cv = cv_ref[...]
    for s in range(ATTN_BLOCKS):
        block = j * ATTN_BLOCKS + s
        a_ref[s * BLOCK:(s + 1) * BLOCK, :] = _attend(
            q_ref[s * BLOCK:(s + 1) * BLOCK, :], sink_ref,
            [ck] + k_blocks[s:s + 3], [cv] + v_blocks[s:s + 3],
            prev_ok=block > 0, next_ok=block < n_blocks - 1)
    first = j == 0
    last = j == n_blocks // ATTN_BLOCKS - 1
    cm_ref[...] = _conv(z_ref[...].astype(F32),
                        jnp.where(first, 0.0, zp_ref[...].astype(F32)),
                        jnp.where(last, 0.0, zn_ref[...].astype(F32)),
                        w_ref, b_ref, gavg_ref, gn_ref, bn_ref, ext_ref, sh_ref, y_ref)


def _conv_specs(layer, ngrid):
    const = (lambda b: (0, 0)) if ngrid == 1 else (lambda b, j: (0, 0))
    per_layer = (lambda b: (layer, 0, 0)) if ngrid == 1 else (lambda b, j: (layer, 0, 0))
    return [
        pl.BlockSpec((None, CONV_WIDTH, D_CONV), per_layer),
        pl.BlockSpec((None, 1, D_CONV), per_layer),
        pl.BlockSpec((D_CONV, D_CONV), const),
        pl.BlockSpec((None, 1, D_CONV), per_layer),
        pl.BlockSpec((None, 1, D_CONV), per_layer),
    ]


def _conv_scratch(tt):
    return [
        pltpu.VMEM((tt + 2 * CONV_HALO, D_CONV), F32),
        pltpu.VMEM((SUBLANES, tt + CONV_SPAN, D_CONV), F32),
        pltpu.VMEM((tt, D_CONV), F32),
    ]


def _mixers_latent(q, k, v, z, cache_k, cache_v, sink, conv_args, layer, batch, seq_len):
    nb = seq_len // BLOCK
    tt = ATTN_BLOCKS * BLOCK
    q3 = q.reshape(batch, seq_len, D_ATTN)
    k3 = k.reshape(batch, seq_len, D_KV)
    v3 = v.reshape(batch, seq_len, D_KV)
    z3 = z.reshape(batch, seq_len, D_CONV)
    past = cache_k.shape[2]
    halo_per_tile = tt // CONV_HALO
    n_halo = seq_len // CONV_HALO
    blk = lambda f: pl.BlockSpec((None, BLOCK, D_KV), f)
    prev_map = lambda b, j: (b, jnp.maximum(j * ATTN_BLOCKS - 1, 0), 0)
    cur_map = lambda b, j: (b, j, 0)
    next_map = lambda b, j: (b, jnp.minimum((j + 1) * ATTN_BLOCKS, nb - 1), 0)
    cur = pl.BlockSpec((None, tt, D_KV), cur_map)
    cache_spec = pl.BlockSpec((None, None, past, D_KV), lambda b, j: (b, layer, 0, 0))
    a, cm = pl.pallas_call(
        functools.partial(_mixer_latent_kernel, n_blocks=nb),
        grid=(batch, nb // ATTN_BLOCKS),
        in_specs=[
            pl.BlockSpec(memory_space=pltpu.SMEM),
            pl.BlockSpec((None, tt, D_ATTN), cur_map),
            cache_spec, cache_spec,
            blk(prev_map), cur, blk(next_map),
            blk(prev_map), cur, blk(next_map),
            pl.BlockSpec((None, tt, D_CONV), cur_map),
            pl.BlockSpec((None, CONV_HALO, D_CONV),
                         lambda b, j: (b, jnp.maximum(j * halo_per_tile - 1, 0), 0)),
            pl.BlockSpec((None, CONV_HALO, D_CONV),
                         lambda b, j: (b, jnp.minimum((j + 1) * halo_per_tile, n_halo - 1), 0)),
        ] + _conv_specs(layer, 2),
        out_specs=[pl.BlockSpec((None, tt, D_ATTN), cur_map), pl.BlockSpec((None, tt, D_CONV), cur_map)],
        out_shape=[jax.ShapeDtypeStruct((batch, seq_len, D_ATTN), BF16),
                   jax.ShapeDtypeStruct((batch, seq_len, D_CONV), BF16)],
        scratch_shapes=_conv_scratch(tt),
        compiler_params=_params("arbitrary", "arbitrary"),
        name="mixer_latent",
    )(sink, q3, cache_k, cache_v, k3, k3, k3, v3, v3, v3, z3, z3, z3, *conv_args)
    return a.reshape(batch * seq_len, D_ATTN), cm.reshape(batch * seq_len, D_CONV)


def _mixers_context(q, k, v, z, sink, conv_args, layer, batch, seq_len):
    q3 = q.reshape(batch, seq_len, D_ATTN)
    k3 = k.reshape(batch, seq_len, D_KV)
    v3 = v.reshape(batch, seq_len, D_KV)
    z3 = z.reshape(batch, seq_len, D_CONV)
    seq = lambda b: (b, 0, 0)
    a, cm = pl.pallas_call(
        _mixer_context_kernel,
        grid=(batch,),
        in_specs=[
            pl.BlockSpec(memory_space=pltpu.SMEM),
            pl.BlockSpec((None, seq_len, D_ATTN), seq),
            pl.BlockSpec((None, seq_len, D_KV), seq),
            pl.BlockSpec((None, seq_len, D_KV), seq),
            pl.BlockSpec((None, seq_len, D_CONV), seq),
        ] + _conv_specs(layer, 1),
        out_specs=[pl.BlockSpec((None, seq_len, D_ATTN), seq), pl.BlockSpec((None, seq_len, D_CONV), seq)],
        out_shape=[jax.ShapeDtypeStruct((batch, seq_len, D_ATTN), BF16),
                   jax.ShapeDtypeStruct((batch, seq_len, D_CONV), BF16)],
        scratch_shapes=_conv_scratch(seq_len),
        compiler_params=_params("arbitrary"),
        name="mixer_context",
    )(sink, q3, k3, v3, z3, *conv_args)
    return a.reshape(batch * seq_len, D_ATTN), cm.reshape(batch * seq_len, D_CONV)


def _outproj_kernel(*refs, routed):
    if routed:
        (a_ref, cm_ref, x_ref, mod_ref, gpost_ref, gpre_ref, w_ref, wr_ref, br_ref, tri_ref,
         x1_ref, h_ref, route_ref, count_ref, carry_ref) = refs
    else:
        a_ref, cm_ref, x_ref, mod_ref, gpost_ref, gpre_ref, w_ref, x1_ref, h_ref = refs
    if routed:
        w_hi, w_lo = _split_bf16(wr_ref[...])
        w_both = jnp.concatenate([w_hi, w_lo], axis=1)
    logit_pieces = []
    for r0 in range(0, x_ref.shape[0], ROW_PIECE):
        rs = slice(r0, r0 + ROW_PIECE)
        m = (jnp.dot(a_ref[rs, :], w_ref[:D_ATTN, :], preferred_element_type=F32)
             + jnp.dot(cm_ref[rs, :], w_ref[D_ATTN:, :], preferred_element_type=F32))
        x1 = x_ref[rs, :] + mod_ref[2:3, :] * _rms(m, gpost_ref[...])
        x1_ref[rs, :] = x1
        h = _rms(x1, gpre_ref[...]) * (1.0 + mod_ref[4:5, :]) + mod_ref[3:4, :]
        h_ref[rs, :] = h.astype(BF16)
        if routed:
            h_hi, h_lo = _split_bf16(h)
            both = jnp.dot(h_hi, w_both, preferred_element_type=F32)
            logit_pieces.append(both[:, :ROUTE_LANES] + both[:, ROUTE_LANES:]
                                + jnp.dot(h_lo, w_hi, preferred_element_type=F32))
    if routed:
        logits = jnp.concatenate(logit_pieces, axis=0) + br_ref[...]
        lane = lax.broadcasted_iota(jnp.int32, logits.shape, 1).astype(F32)
        logits = jnp.where(lane < N_EXPERTS, logits, -jnp.inf)
        m1 = jnp.max(logits, axis=-1, keepdims=True)
        i1 = jnp.min(jnp.where(logits == m1, lane, float(ROUTE_LANES)), axis=-1, keepdims=True)
        rest = jnp.where(lane == i1, -jnp.inf, logits)
        m2 = jnp.max(rest, axis=-1, keepdims=True)
        i2 = jnp.min(jnp.where(rest == m2, lane, float(ROUTE_LANES)), axis=-1, keepdims=True)
        e2 = jnp.exp(m2 - m1)
        g1 = 1.0 / (1.0 + e2)
        g2 = e2 * g1

        @pl.when(pl.program_id(0) == 0)
        def _():
            carry_ref[...] = jnp.zeros_like(carry_ref)

        hit1 = lane == i1
        hit2 = lane == i2
        picked = jnp.where(hit1, 1.0, jnp.where(hit2, 1.0, 0.0))
        before = carry_ref[...] + jnp.dot(tri_ref[...], picked.astype(BF16), preferred_element_type=F32)
        rank1 = jnp.sum(jnp.where(hit1, before, 0.0), axis=-1, keepdims=True)
        rank2 = jnp.sum(jnp.where(hit2, before, 0.0), axis=-1, keepdims=True)
        total = carry_ref[...] + jnp.sum(picked, axis=0, keepdims=True)
        carry_ref[...] = total
        count_ref[...] = total
        cols = (i1, i2, g1, g2, rank1, rank2)
        route = jnp.zeros_like(logits)
        for c, val in enumerate(cols):
            route = jnp.where(lane == c, val, route)
        route_ref[...] = route


def _outproj(a, cm, x, mods, gpost, gpre, w_out, layer, router):
    n = x.shape[0]
    tm = TOKEN_TILE
    tiles_per_mod = (n // mods.shape[0]) // tm
    row = lambda i: (i, 0)
    const = lambda i: (0, 0)
    in_specs = [
        pl.BlockSpec((tm, D_ATTN), row),
        pl.BlockSpec((tm, D_CONV), row),
        pl.BlockSpec((tm, D_MODEL), row),
        pl.BlockSpec((None, 6, D_MODEL), lambda i: (i // tiles_per_mod, 0, 0)),
        pl.BlockSpec((1, D_MODEL), const),
        pl.BlockSpec((1, D_MODEL), const),
        pl.BlockSpec((None, D_MODEL, D_MODEL), lambda i: (layer, 0, 0)),
    ]
    args = [a, cm, x, mods, gpost, gpre, w_out]
    out_specs = [pl.BlockSpec((tm, D_MODEL), row), pl.BlockSpec((tm, D_MODEL), row)]
    out_shape = [jax.ShapeDtypeStruct((n, D_MODEL), F32), jax.ShapeDtypeStruct((n, D_MODEL), BF16)]
    scratch = []
    if router is not None:
        in_specs += [pl.BlockSpec((D_MODEL, ROUTE_LANES), const), pl.BlockSpec((1, ROUTE_LANES), const),
                     pl.BlockSpec((tm, tm), const)]
        args += list(router)
        out_specs += [pl.BlockSpec((tm, ROUTE_LANES), row), pl.BlockSpec((1, ROUTE_LANES), const)]
        out_shape += [jax.ShapeDtypeStruct((n, ROUTE_LANES), F32),
                      jax.ShapeDtypeStruct((1, ROUTE_LANES), F32)]
        scratch = [pltpu.VMEM((1, ROUTE_LANES), F32)]
    return pl.pallas_call(
        functools.partial(_outproj_kernel, routed=router is not None),
        grid=(n // tm,),
        in_specs=in_specs,
        out_specs=out_specs,
        out_shape=out_shape,
        scratch_shapes=scratch,
        compiler_params=_params("arbitrary"),
        name="outproj",
    )(*args)


def _swiglu_hidden(h, wg_ref, wu_ref, act_ref, up_col, width):
    for c in range(0, width, FF_CHUNK):
        g = jnp.dot(h, wg_ref[:, c:c + FF_CHUNK], preferred_element_type=F32)
        u = jnp.dot(h, wu_ref[:, up_col + c:up_col + c + FF_CHUNK], preferred_element_type=F32)
        act_ref[:, c:c + FF_CHUNK] = (_silu(g) * u).astype(BF16)


def _ffn_kernel(h_ref, x_ref, mod_ref, g_ref, wgu_ref, wd_ref, o_ref, act_ref):
    _swiglu_hidden(h_ref[...], wgu_ref, wgu_ref, act_ref, D_FF, D_FF)
    f = jnp.dot(act_ref[...], wd_ref[...], preferred_element_type=F32)
    o_ref[...] = x_ref[...] + mod_ref[5:6, :] * _rms(f, g_ref[...])


def _ffn_dense(h, x, mods, g, w_gu, w_down, index):
    n = x.shape[0]
    tm = FFN_TILE
    tiles_per_mod = (n // mods.shape[0]) // tm
    row = lambda i: (i, 0)
    whole = lambda i: (index, 0, 0)
    resident = pl.Buffered(1)
    return pl.pallas_call(
        _ffn_kernel,
        grid=(n // tm,),
        in_specs=[
            pl.BlockSpec((tm, D_MODEL), row),
            pl.BlockSpec((tm, D_MODEL), row),
            pl.BlockSpec((None, 6, D_MODEL), lambda i: (i // tiles_per_mod, 0, 0)),
            pl.BlockSpec((1, D_MODEL), lambda i: (0, 0)),
            pl.BlockSpec((None, D_MODEL, 2 * D_FF), whole, pipeline_mode=resident),
            pl.BlockSpec((None, D_FF, D_MODEL), whole, pipeline_mode=resident),
        ],
        out_specs=pl.BlockSpec((tm, D_MODEL), row),
        out_shape=jax.ShapeDtypeStruct((n, D_MODEL), F32),
        scratch_shapes=[pltpu.VMEM((tm, D_FF), BF16)],
        compiler_params=_params("arbitrary"),
        name="ffn_dense",
    )(h, x, mods, g, w_gu, w_down)


def _moe_kernel(te_ref, na_ref, h_ref, wg_ref, wu_ref, wd_ref, o_ref, act_ref, acc_ref):
    i = pl.program_id(0)
    j = pl.program_id(1)

    @pl.when(i < na_ref[0])
    def _():
        _swiglu_hidden(h_ref[...], wg_ref, wu_ref, act_ref, 0, MOE_CHUNK)
        part = jnp.dot(act_ref[...], wd_ref[...], preferred_element_type=F32)

        @pl.when(j == 0)
        def _():
            acc_ref[...] = part

        @pl.when((j > 0) & (j < N_MOE_CHUNKS - 1))
        def _():
            acc_ref[...] += part

        @pl.when(j == N_MOE_CHUNKS - 1)
        def _():
            o_ref[...] = (acc_ref[...] + part).astype(BF16)


def _moe_experts(hs, tile_expert, n_active, w_gu, w_down, index):
    rows = hs.shape[0]
    tm = MOE_TILE

    def tile(i, na):
        return jnp.minimum(i, na[0] - 1)

    def chunk(i, j, na):
        return jnp.where(i < na[0], j, N_MOE_CHUNKS - 1)

    grid_spec = pltpu.PrefetchScalarGridSpec(
        num_scalar_prefetch=2,
        grid=(rows // tm, N_MOE_CHUNKS),
        in_specs=[
            pl.BlockSpec((tm, D_MODEL), lambda i, j, te, na: (tile(i, na), 0)),
            pl.BlockSpec((None, None, D_MODEL, MOE_CHUNK),
                         lambda i, j, te, na: (index, te[tile(i, na)], 0, chunk(i, j, na))),
            pl.BlockSpec((None, None, D_MODEL, MOE_CHUNK),
                         lambda i, j, te, na: (index, te[tile(i, na)], 0, N_MOE_CHUNKS + chunk(i, j, na))),
            pl.BlockSpec((None, None, MOE_CHUNK, D_MODEL),
                         lambda i, j, te, na: (index, te[tile(i, na)], chunk(i, j, na), 0)),
        ],
        out_specs=pl.BlockSpec((tm, D_MODEL), lambda i, j, te, na: (tile(i, na), 0)),
        scratch_shapes=[pltpu.VMEM((tm, MOE_CHUNK), BF16), pltpu.VMEM((tm, D_MODEL), F32)],
    )
    return pl.pallas_call(
        _moe_kernel,
        grid_spec=grid_spec,
        out_shape=jax.ShapeDtypeStruct((rows, D_MODEL), BF16),
        compiler_params=_params("arbitrary", "arbitrary"),
        name="moe_experts",
    )(tile_expert, n_active, hs, w_gu, w_gu, w_down)


def _combine_kernel(y1_ref, y2_ref, route_ref, x_ref, mod_ref, g_ref, o_ref):
    f = (route_ref[:, 2:3] * y1_ref[...].astype(F32) + route_ref[:, 3:4] * y2_ref[...].astype(F32))
    o_ref[...] = x_ref[...] + mod_ref[5:6, :] * _rms(f, g_ref[...])


def _moe_combine(y1, y2, route, x, mods, g):
    n = x.shape[0]
    tm = TOKEN_TILE
    tiles_per_mod = (n // mods.shape[0]) // tm
    row = lambda i: (i, 0)
    return pl.pallas_call(
        _combine_kernel,
        grid=(n // tm,),
        in_specs=[
            pl.BlockSpec((tm, D_MODEL), row),
            pl.BlockSpec((tm, D_MODEL), row),
            pl.BlockSpec((tm, ROUTE_LANES), row),
            pl.BlockSpec((tm, D_MODEL), row),
            pl.BlockSpec((None, 6, D_MODEL), lambda i: (i // tiles_per_mod, 0, 0)),
            pl.BlockSpec((1, D_MODEL), lambda i: (0, 0)),
        ],
        out_specs=pl.BlockSpec((tm, D_MODEL), row),
        out_shape=jax.ShapeDtypeStruct((n, D_MODEL), F32),
        compiler_params=_params("arbitrary"),
        name="moe_combine",
    )(y1, y2, route, x, mods, g)


def _moe(h, x, route, counts, mods, g, w_gu, w_down, index):
    n = x.shape[0]
    tm = MOE_TILE
    rows = 2 * n + N_EXPERTS * tm
    counts = counts[0, :N_EXPERTS].astype(jnp.int32)
    padded = ((counts + tm - 1) // tm) * tm
    ends = jnp.cumsum(padded)
    starts = ends - padded
    sel = route[:, 0:6].astype(jnp.int32)
    pos1 = _rows(starts, sel[:, 0]) + sel[:, 4]
    pos2 = _rows(starts, sel[:, 1]) + sel[:, 5]
    token = jnp.arange(n, dtype=jnp.int32)
    src = jnp.zeros((rows,), jnp.int32).at[jnp.concatenate([pos1, pos2])].set(
        jnp.concatenate([token, token]), mode="promise_in_bounds", unique_indices=True)
    tile_start = jnp.arange(rows // tm, dtype=jnp.int32) * tm
    tile_expert = jnp.minimum(
        jnp.sum((tile_start[:, None] >= ends[None, :]).astype(jnp.int32), axis=1), N_EXPERTS - 1)
    n_active = (ends[-1:] // tm).astype(jnp.int32)
    ys = _moe_experts(_rows(h, src), tile_expert, n_active, w_gu, w_down, index)
    return _moe_combine(_rows(ys, pos1), _rows(ys, pos2), route, x, mods, g)


def kernel(x_prompt, x_sample, c, c_ctx, cache_k, cache_v, w_mod, b_mod, g_pre_mix, g_post_mix,
           g_pre_ffn, g_post_ffn, w_in, w_out, attn_sink, conv_w, conv_b, conv_norm_g, conv_norm_b,
           ffn_w_gu, ffn_w_down, moe_w_router, moe_b_router, moe_w_gu, moe_w_down):
    batch, seq, _ = x_prompt.shape
    dec_batch, dec_seq, _ = x_sample.shape
    past = cache_k.shape[2]

    cvec = jnp.zeros((MOD_ROWS, D_MODEL), F32).at[0].set(c_ctx).at[1:1 + dec_batch].set(c)
    mods = _modulations(cvec, w_mod, b_mod)

    w_in_b = jnp.concatenate([_pair_heads(w_in[:, :, :D_ATTN], 2), w_in[:, :, D_ATTN:]], axis=2).astype(BF16)
    w_out_b = jnp.concatenate([_pair_heads(w_out[:, :D_ATTN, :], 1), w_out[:, D_ATTN:, :]], axis=1).astype(BF16)
    ffn_gu_b = ffn_w_gu.astype(BF16)
    ffn_down_b = ffn_w_down.astype(BF16)
    moe_gu_b = _to_bf16(moe_w_gu)
    moe_down_b = _to_bf16(moe_w_down)
    w_router = jnp.zeros((DEPTH // 2, D_MODEL, ROUTE_LANES), F32).at[:, :, :N_EXPERTS].set(moe_w_router)
    b_router = jnp.zeros((DEPTH // 2, 1, ROUTE_LANES), F32).at[:, 0, :N_EXPERTS].set(moe_b_router)
    sink2 = attn_sink * LOG2E
    cache_k4 = cache_k.reshape(dec_batch, DEPTH, past, D_KV)
    cache_v4 = cache_v.reshape(dec_batch, DEPTH, past, D_KV)
    rope_tabs = _rope_tables(dec_seq)
    gavg = jnp.kron(jnp.eye(CONV_GROUPS, dtype=F32),
                    jnp.full((D_CONV // CONV_GROUPS,) * 2, CONV_GROUPS / D_CONV, F32)).astype(BF16)
    tri = jnp.tril(jnp.ones((TOKEN_TILE, TOKEN_TILE), F32), -1).astype(BF16)
    conv_args = (conv_w, conv_b[:, None, :], gavg, conv_norm_g[:, None, :], conv_norm_b[:, None, :])

    def layer(l, x, mods_l, latent):
        n_seq, seq_len = (dec_batch, dec_seq) if latent else (batch, seq)
        row = lambda v: v[l][None, :]
        q, k, v, z = _inproj(x, mods_l, row(g_pre_mix), w_in_b, l, rope_tabs if latent else None, seq_len)
        if latent:
            a, cm = _mixers_latent(q, k, v, z, cache_k4, cache_v4, sink2[l], conv_args, l, n_seq, seq_len)
        else:
            a, cm = _mixers_context(q, k, v, z, sink2[l], conv_args, l, n_seq, seq_len)
        if l % 2 == 0:
            x1, h = _outproj(a, cm, x, mods_l, row(g_post_mix), row(g_pre_ffn), w_out_b, l, None)
            x2 = _ffn_dense(h, x1, mods_l, row(g_post_ffn), ffn_gu_b, ffn_down_b, l // 2)
        else:
            x1, h, route, counts = _outproj(a, cm, x, mods_l, row(g_post_mix), row(g_pre_ffn), w_out_b, l,
                                            (w_router[l // 2], b_router[l // 2], tri))
            x2 = _moe(h, x1, route, counts, mods_l, row(g_post_ffn), moe_gu_b, moe_down_b, l // 2)
        return x2, k, v

    xp = x_prompt.reshape(batch * seq, D_MODEL)
    ks, vs = [], []
    for l in range(DEPTH):
        xp, k_l, v_l = layer(l, xp, mods[l, 0:1], False)
        ks.append(k_l.reshape(batch, seq, N_KV_HEADS, HEAD_DIM))
        vs.append(v_l.reshape(batch, seq, N_KV_HEADS, HEAD_DIM))

    xs = x_sample.reshape(dec_batch * dec_seq, D_MODEL)
    for l in range(DEPTH):
        xs, _, _ = layer(l, xs, mods[l, 1:1 + dec_batch], True)

    return (xp.reshape(batch, seq, D_MODEL), xs.reshape(dec_batch, dec_seq, D_MODEL),
            jnp.stack(ks, axis=1), jnp.stack(vs, axis=1))
```

```python
import functools
import math

import jax
import jax.numpy as jnp
from jax import lax
from jax.experimental import pallas as pl
from jax.experimental.pallas import tpu as pltpu

F32 = jnp.float32
BF16 = jnp.bfloat16

D_MODEL = 1024
DEPTH = 4
GRID_W = 64
D_ATTN = 512
D_CONV = 512
HEAD_DIM = 64
N_HEADS = 8
N_KV_HEADS = 2
N_REP = N_HEADS // N_KV_HEADS
D_KV = N_KV_HEADS * HEAD_DIM
D_IN = D_ATTN + 2 * D_KV + 2 * D_CONV
WINDOW = 128
BLOCK = 128
ATTN_SCALE = HEAD_DIM ** -0.5
ROPE_BASE = 10000.0
CONV_WIDTH = 31
CONV_HALF = CONV_WIDTH // 2
CONV_GROUPS = 8
D_FF = 2816
N_EXPERTS = 8
D_FF_EXPERT = 3584
EPS = 1e-6
NEG_INF = -1e30
LOG2E = math.log2(math.e)

LANES = 128
SUBLANES = 8
VMEM_LIMIT = 48 * 1024 * 1024

TOKEN_TILE = 1024
ROW_PIECE = 512
FFN_TILE = 512
ATTN_BLOCKS = 4
CONV_HALO = 16
CONV_ROWS = 128
CONV_SPAN = SUBLANES * ((CONV_WIDTH - 1) // SUBLANES)
FF_CHUNK = 256
MOE_TILE = 512
DISPATCH_TILE = 1024
MOE_CHUNK = D_FF_EXPERT // 2
N_MOE_CHUNKS = D_FF_EXPERT // MOE_CHUNK
MOD_ROWS = 16
ROUTE_LANES = 128
CAST_BLOCK_BYTES = 8 * 1024 * 1024
CAST_COLS = 1024

assert N_KV_HEADS * HEAD_DIM == LANES and D_ATTN == N_REP * LANES
assert WINDOW == BLOCK


def _params(*sem):
    return pltpu.CompilerParams(dimension_semantics=sem, vmem_limit_bytes=VMEM_LIMIT)


def _silu(x):
    return x * jax.nn.sigmoid(x)


def _rms(x, g):
    return x * lax.rsqrt(jnp.mean(x * x, axis=-1, keepdims=True) + EPS) * g


def _split_bf16(x):
    hi = x.astype(BF16)
    lo = (x - hi.astype(F32)).astype(BF16)
    return hi, lo


def _rows(arr, idx):
    return arr.at[idx].get(mode="promise_in_bounds")


def _cast_kernel(x_ref, o_ref):
    o_ref[...] = x_ref[...].astype(BF16)


def _to_bf16(w):
    shape = w.shape
    cols = shape[-1]
    w2 = w.reshape(-1, cols)
    rows = w2.shape[0]
    block = CAST_BLOCK_BYTES // (4 * CAST_COLS)
    assert rows % block == 0 and cols % CAST_COLS == 0
    out = pl.pallas_call(
        _cast_kernel,
        grid=(rows // block, cols // CAST_COLS),
        in_specs=[pl.BlockSpec((block, CAST_COLS), lambda i, j: (i, j))],
        out_specs=pl.BlockSpec((block, CAST_COLS), lambda i, j: (i, j)),
        out_shape=jax.ShapeDtypeStruct((rows, cols), BF16),
        compiler_params=_params("arbitrary", "arbitrary"),
        name="weight_cast",
    )(w2)
    return out.reshape(shape)


def _mod_kernel(c_ref, w_ref, b_ref, o_ref):
    a = _silu(c_ref[...]).astype(BF16)
    o_ref[...] = jnp.dot(a, w_ref[...].astype(BF16), preferred_element_type=F32) + b_ref[...]


def _modulations(cvec, w_mod, b_mod):
    out = pl.pallas_call(
        _mod_kernel,
        grid=(DEPTH, 6),
        in_specs=[
            pl.BlockSpec((MOD_ROWS, D_MODEL), lambda l, s: (0, 0)),
            pl.BlockSpec((None, D_MODEL, D_MODEL), lambda l, s: (l, 0, s)),
            pl.BlockSpec((None, 1, D_MODEL), lambda l, s: (l, 0, s)),
        ],
        out_specs=pl.BlockSpec((None, MOD_ROWS, D_MODEL), lambda l, s: (l, 0, s)),
        out_shape=jax.ShapeDtypeStruct((DEPTH, MOD_ROWS, 6 * D_MODEL), F32),
        compiler_params=_params("arbitrary", "arbitrary"),
        name="modulations",
    )(cvec, w_mod, b_mod.reshape(DEPTH, 1, 6 * D_MODEL))
    return out.reshape(DEPTH, MOD_ROWS, 6, D_MODEL)


def _rope(x, cos, sin):
    lane = lax.broadcasted_iota(jnp.int32, x.shape, 1)
    width = x.shape[1]
    partner = jnp.where((lane & 16) == 0, pltpu.roll(x, width - 16, 1), pltpu.roll(x, 16, 1))
    return x * cos + partner * sin


def _inproj_kernel(*refs, rope):
    if rope:
        x_ref, mod_ref, g_ref, w_ref, cos_ref, sin_ref, q_ref, k_ref, v_ref, z_ref = refs
    else:
        x_ref, mod_ref, g_ref, w_ref, q_ref, k_ref, v_ref, z_ref = refs
    for r0 in range(0, x_ref.shape[0], ROW_PIECE):
        rs = slice(r0, r0 + ROW_PIECE)
        h = _rms(x_ref[rs, :], g_ref[...]) * (1.0 + mod_ref[1:2, :]) + mod_ref[0:1, :]
        p = jnp.dot(h.astype(BF16), w_ref[...], preferred_element_type=F32)
        q = p[:, :D_ATTN]
        k = p[:, D_ATTN:D_ATTN + D_KV]
        if rope:
            cos = cos_ref[rs, :]
            sin = sin_ref[rs, :]
            q = _rope(q, jnp.concatenate([cos] * N_REP, axis=1), jnp.concatenate([sin] * N_REP, axis=1))
            k = _rope(k, cos, sin)
        q_ref[rs, :] = (q * (ATTN_SCALE * LOG2E)).astype(BF16)
        k_ref[rs, :] = k
        v_ref[rs, :] = p[:, D_ATTN + D_KV:D_ATTN + 2 * D_KV]
        a = p[:, D_ATTN + 2 * D_KV:D_ATTN + 2 * D_KV + D_CONV]
        gt = p[:, D_ATTN + 2 * D_KV + D_CONV:]
        z_ref[rs, :] = (a * jax.nn.sigmoid(gt)).astype(BF16)


def _inproj(x, mods, g, w_in, layer, rope_tabs, seq_len):
    n = x.shape[0]
    tm = TOKEN_TILE
    tiles_per_mod = (n // mods.shape[0]) // tm
    row = lambda i: (i, 0)
    const = lambda i: (0, 0)
    in_specs = [
        pl.BlockSpec((tm, D_MODEL), row),
        pl.BlockSpec((None, 6, D_MODEL), lambda i: (i // tiles_per_mod, 0, 0)),
        pl.BlockSpec((1, D_MODEL), const),
        pl.BlockSpec((None, D_MODEL, D_IN), lambda i: (layer, 0, 0)),
    ]
    args = [x, mods, g, w_in]
    if rope_tabs is not None:
        tiles_per_seq = seq_len // tm
        tab = pl.BlockSpec((tm, LANES), lambda i: (i % tiles_per_seq, 0))
        in_specs += [tab, tab]
        args += list(rope_tabs)
    return pl.pallas_call(
        functools.partial(_inproj_kernel, rope=rope_tabs is not None),
        grid=(n // tm,),
        in_specs=in_specs,
        out_specs=[
            pl.BlockSpec((tm, D_ATTN), row),
            pl.BlockSpec((tm, D_KV), row),
            pl.BlockSpec((tm, D_KV), row),
            pl.BlockSpec((tm, D_CONV), row),
        ],
        out_shape=[
            jax.ShapeDtypeStruct((n, D_ATTN), BF16),
            jax.ShapeDtypeStruct((n, D_KV), F32),
            jax.ShapeDtypeStruct((n, D_KV), F32),
            jax.ShapeDtypeStruct((n, D_CONV), BF16),
        ],
        compiler_params=_params("arbitrary"),
        name="inproj",
    )(*args)


def _rope_tables(seq_len):
    half = HEAD_DIM // 2
    n_freq = half // 2
    t = jnp.arange(seq_len)
    inv = ROPE_BASE ** (-jnp.arange(n_freq, dtype=F32) * 2.0 / half)
    ang_r = (t // GRID_W).astype(F32)[:, None] * inv[None, :]
    ang_c = (t % GRID_W).astype(F32)[:, None] * inv[None, :]
    cos = jnp.concatenate([jnp.cos(ang_r)] * 2 + [jnp.cos(ang_c)] * 2, axis=-1)
    sin = jnp.concatenate([-jnp.sin(ang_r), jnp.sin(ang_r), -jnp.sin(ang_c), jnp.sin(ang_c)], axis=-1)
    return jnp.tile(cos, (1, LANES // HEAD_DIM)), jnp.tile(sin, (1, LANES // HEAD_DIM))


def _pair_heads(w, axis):
    shape = w.shape
    split = shape[:axis] + (N_KV_HEADS, N_REP, HEAD_DIM) + shape[axis + 1:]
    return jnp.swapaxes(w.reshape(split), axis, axis + 1).reshape(shape)


def _attend(q, sink_ref, ks, vs, prev_ok=None, next_ok=None):
    qb = q.shape[0]
    past = ks[0].shape[0]
    windowed = len(ks) > 1
    low = lax.broadcasted_iota(jnp.int32, (qb, LANES), 1) < HEAD_DIM
    zero = jnp.zeros((qb, LANES), BF16)
    blocks, sinks = [], []
    for g in range(N_REP):
        slab = q[:, g * LANES:(g + 1) * LANES]
        blocks += [jnp.where(low, slab, zero), jnp.where(low, zero, slab)]
        sinks += [jnp.full((qb, 1), sink_ref[g], F32), jnp.full((qb, 1), sink_ref[N_REP + g], F32)]
    qz = jnp.concatenate(blocks, axis=0)
    sink = jnp.concatenate(sinks, axis=0)
    k_all = jnp.concatenate(ks, axis=0).astype(BF16)
    v_all = jnp.concatenate(vs, axis=0).astype(BF16)
    v_ext = jnp.concatenate([v_all, jnp.ones_like(v_all)], axis=1)
    s = lax.dot_general(qz, k_all, (((1,), (1,)), ((), ())), preferred_element_type=F32)
    pieces = [s[:, c:c + LANES] for c in range(0, past, LANES)]
    if windowed:
        qi = lax.broadcasted_iota(jnp.int32, (qb, LANES), 0)
        kk = lax.broadcasted_iota(jnp.int32, (qb, LANES), 1)
        bias_prev = jnp.where((kk >= qi) & prev_ok, 0.0, NEG_INF)
        bias_next = jnp.where((kk <= qi) & next_ok, 0.0, NEG_INF)
        rep = lambda b: jnp.concatenate([b] * N_HEADS, axis=0)
        pieces += [s[:, past:past + BLOCK] + rep(bias_prev),
                   s[:, past + BLOCK:past + 2 * BLOCK],
                   s[:, past + 2 * BLOCK:] + rep(bias_next)]
    m = jnp.maximum(jnp.max(functools.reduce(jnp.maximum, pieces), axis=-1, keepdims=True), sink)
    p = jnp.concatenate([jnp.exp2(x - m).astype(BF16) for x in pieces], axis=1)
    o = jnp.dot(p, v_ext, preferred_element_type=F32)
    r = o[:, :LANES] / (o[:, LANES:] + jnp.exp2(sink - m))
    slabs = [jnp.where(low, r[2 * g * qb:(2 * g + 1) * qb], r[(2 * g + 1) * qb:(2 * g + 2) * qb])
             for g in range(N_REP)]
    return jnp.concatenate(slabs, axis=1).astype(BF16)


def _conv(z, halo_prev, halo_next, w_ref, b_ref, gavg_ref, gn_ref, bn_ref, ext_ref, sh_ref, y_ref):
    tt = z.shape[0]
    ext_ref[0:CONV_HALO, :] = halo_prev
    ext_ref[CONV_HALO:CONV_HALO + tt, :] = z
    ext_ref[CONV_HALO + tt:, :] = halo_next
    base = CONV_HALO - CONV_HALF
    for r in range(SUBLANES):
        sh_ref[r] = ext_ref[base + r:base + r + tt + CONV_SPAN, :]
    for c in range(D_CONV // LANES):
        cs = slice(c * LANES, (c + 1) * LANES)
        w = w_ref[:, cs]
        for rb in range(tt // CONV_ROWS):
            r0 = rb * CONV_ROWS
            acc = jnp.broadcast_to(b_ref[:, cs], (CONV_ROWS, LANES))
            for j in range(CONV_WIDTH):
                r, off = j % SUBLANES, r0 + SUBLANES * (j // SUBLANES)
                acc = acc + w[j:j + 1, :] * sh_ref[r, off:off + CONV_ROWS, cs]
            y_ref[r0:r0 + CONV_ROWS, cs] = acc
    y = y_ref[...]
    gavg = gavg_ref[...]

    def group_mean(a):
        hi, lo = _split_bf16(a)
        return (jnp.dot(hi, gavg, preferred_element_type=F32)
                + jnp.dot(lo, gavg, preferred_element_type=F32))

    d = y - group_mean(y)
    var = group_mean(d * d)
    zn = d * lax.rsqrt(var + EPS) * gn_ref[...] + bn_ref[...]
    return _silu(zn).astype(BF16)


def _mixer_context_kernel(sink_ref, q_ref, k_ref, v_ref, z_ref, w_ref, b_ref, gavg_ref, gn_ref, bn_ref,
                          a_ref, cm_ref, ext_ref, sh_ref, y_ref):
    a_ref[...] = _attend(q_ref[...], sink_ref, [k_ref[...]], [v_ref[...]])
    edge = jnp.zeros((CONV_HALO, D_CONV), F32)
    cm_ref[...] = _conv(z_ref[...].astype(F32), edge, edge, w_ref, b_ref, gavg_ref, gn_ref, bn_ref,
                        ext_ref, sh_ref, y_ref)


def _mixer_latent_kernel(sink_ref, q_ref, ck_ref, cv_ref, kp_ref, kc_ref, kn_ref, vp_ref, vc_ref, vn_ref,
                         z_ref, zp_ref, zn_ref, w_ref, b_ref, gavg_ref, gn_ref, bn_ref,
                         a_ref, cm_ref, ext_ref, sh_ref, y_ref, *, n_blocks):
    j = pl.program_id(1)
    k_blocks = [kp_ref[...]] + [kc_ref[s * BLOCK:(s + 1) * BLOCK, :] for s in range(ATTN_BLOCKS)] + [kn_ref[...]]
    v_blocks = [vp_ref[...]] + [vc_ref[s * BLOCK:(s + 1) * BLOCK, :] for s in range(ATTN_BLOCKS)] + [vn_ref[...]]
    ck = ck_ref[...]
    cv = cv_ref[...]
    for s in range(ATTN_BLOCKS):
        block = j * ATTN_BLOCKS + s
        a_ref[s * BLOCK:(s + 1) * BLOCK, :] = _attend(
            q_ref[s * BLOCK:(s + 1) * BLOCK, :], sink_ref,
            [ck] + k_blocks[s:s + 3], [cv] + v_blocks[s:s + 3],
            prev_ok=block > 0, next_ok=block < n_blocks - 1)
    first = j == 0
    last = j == n_blocks // ATTN_BLOCKS - 1
    cm_ref[...] = _conv(z_ref[...].astype(F32),
                        jnp.where(first, 0.0, zp_ref[...].astype(F32)),
                        jnp.where(last, 0.0, zn_ref[...].astype(F32)),
                        w_ref, b_ref, gavg_ref, gn_ref, bn_ref, ext_ref, sh_ref, y_ref)


def _conv_specs(layer, ngrid):
    const = (lambda b: (0, 0)) if ngrid == 1 else (lambda b, j: (0, 0))
    per_layer = (lambda b: (layer, 0, 0)) if ngrid == 1 else (lambda b, j: (layer, 0, 0))
    return [
        pl.BlockSpec((None, CONV_WIDTH, D_CONV), per_layer),
        pl.BlockSpec((None, 1, D_CONV), per_layer),
        pl.BlockSpec((D_CONV, D_CONV), const),
        pl.BlockSpec((None, 1, D_CONV), per_layer),
        pl.BlockSpec((None, 1, D_CONV), per_layer),
    ]


def _conv_scratch(tt):
    return [
        pltpu.VMEM((tt + 2 * CONV_HALO, D_CONV), F32),
        pltpu.VMEM((SUBLANES, tt + CONV_SPAN, D_CONV), F32),
        pltpu.VMEM((tt, D_CONV), F32),
    ]


def _mixers_latent(q, k, v, z, cache_k, cache_v, sink, conv_args, layer, batch, seq_len):
    nb = seq_len // BLOCK
    tt = ATTN_BLOCKS * BLOCK
    q3 = q.reshape(batch, seq_len, D_ATTN)
    k3 = k.reshape(batch, seq_len, D_KV)
    v3 = v.reshape(batch, seq_len, D_KV)
    z3 = z.reshape(batch, seq_len, D_CONV)
    past = cache_k.shape[2]
    halo_per_tile = tt // CONV_HALO
    n_halo = seq_len // CONV_HALO
    blk = lambda f: pl.BlockSpec((None, BLOCK, D_KV), f)
    prev_map = lambda b, j: (b, jnp.maximum(j * ATTN_BLOCKS - 1, 0), 0)
    cur_map = lambda b, j: (b, j, 0)
    next_map = lambda b, j: (b, jnp.minimum((j + 1) * ATTN_BLOCKS, nb - 1), 0)
    cur = pl.BlockSpec((None, tt, D_KV), cur_map)
    cache_spec = pl.BlockSpec((None, None, past, D_KV), lambda b, j: (b, layer, 0, 0))
    a, cm = pl.pallas_call(
        functools.partial(_mixer_latent_kernel, n_blocks=nb),
        grid=(batch, nb // ATTN_BLOCKS),
        in_specs=[
            pl.BlockSpec(memory_space=pltpu.SMEM),
            pl.BlockSpec((None, tt, D_ATTN), cur_map),
            cache_spec, cache_spec,
            blk(prev_map), cur, blk(next_map),
            blk(prev_map), cur, blk(next_map),
            pl.BlockSpec((None, tt, D_CONV), cur_map),
            pl.BlockSpec((None, CONV_HALO, D_CONV),
                         lambda b, j: (b, jnp.maximum(j * halo_per_tile - 1, 0), 0)),
            pl.BlockSpec((None, CONV_HALO, D_CONV),
                         lambda b, j: (b, jnp.minimum((j + 1) * halo_per_tile, n_halo - 1), 0)),
        ] + _conv_specs(layer, 2),
        out_specs=[pl.BlockSpec((None, tt, D_ATTN), cur_map), pl.BlockSpec((None, tt, D_CONV), cur_map)],
        out_shape=[jax.ShapeDtypeStruct((batch, seq_len, D_ATTN), BF16),
                   jax.ShapeDtypeStruct((batch, seq_len, D_CONV), BF16)],
        scratch_shapes=_conv_scratch(tt),
        compiler_params=_params("arbitrary", "arbitrary"),
        name="mixer_latent",
    )(sink, q3, cache_k, cache_v, k3, k3, k3, v3, v3, v3, z3, z3, z3, *conv_args)
    return a.reshape(batch * seq_len, D_ATTN), cm.reshape(batch * seq_len, D_CONV)


def _mixers_context(q, k, v, z, sink, conv_args, layer, batch, seq_len):
    q3 = q.reshape(batch, seq_len, D_ATTN)
    k3 = k.reshape(batch, seq_len, D_KV)
    v3 = v.reshape(batch, seq_len, D_KV)
    z3 = z.reshape(batch, seq_len, D_CONV)
    seq = lambda b: (b, 0, 0)
    a, cm = pl.pallas_call(
        _mixer_context_kernel,
        grid=(batch,),
        in_specs=[
            pl.BlockSpec(memory_space=pltpu.SMEM),
            pl.BlockSpec((None, seq_len, D_ATTN), seq),
            pl.BlockSpec((None, seq_len, D_KV), seq),
            pl.BlockSpec((None, seq_len, D_KV), seq),
            pl.BlockSpec((None, seq_len, D_CONV), seq),
        ] + _conv_specs(layer, 1),
        out_specs=[pl.BlockSpec((None, seq_len, D_ATTN), seq), pl.BlockSpec((None, seq_len, D_CONV), seq)],
        out_shape=[jax.ShapeDtypeStruct((batch, seq_len, D_ATTN), BF16),
                   jax.ShapeDtypeStruct((batch, seq_len, D_CONV), BF16)],
        scratch_shapes=_conv_scratch(seq_len),
        compiler_params=_params("arbitrary"),
        name="mixer_context",
    )(sink, q3, k3, v3, z3, *conv_args)
    return a.reshape(batch * seq_len, D_ATTN), cm.reshape(batch * seq_len, D_CONV)


def _outproj_kernel(*refs, routed):
    if routed:
        (a_ref, cm_ref, x_ref, mod_ref, gpost_ref, gpre_ref, w_ref, wr_ref, br_ref, tri_ref,
         x1_ref, h_ref, route_ref, count_ref, carry_ref) = refs
    else:
        a_ref, cm_ref, x_ref, mod_ref, gpost_ref, gpre_ref, w_ref, x1_ref, h_ref = refs
    if routed:
        w_hi, w_lo = _split_bf16(wr_ref[...])
        w_both = jnp.concatenate([w_hi, w_lo], axis=1)
    logit_pieces = []
    for r0 in range(0, x_ref.shape[0], ROW_PIECE):
        rs = slice(r0, r0 + ROW_PIECE)
        m = (jnp.dot(a_ref[rs, :], w_ref[:D_ATTN, :], preferred_element_type=F32)
             + jnp.dot(cm_ref[rs, :], w_ref[D_ATTN:, :], preferred_element_type=F32))
        x1 = x_ref[rs, :] + mod_ref[2:3, :] * _rms(m, gpost_ref[...])
        x1_ref[rs, :] = x1
        h = _rms(x1, gpre_ref[...]) * (1.0 + mod_ref[4:5, :]) + mod_ref[3:4, :]
        if not routed:
            h_ref[rs, :] = h.astype(BF16)
        else:
            h_hi, h_lo = _split_bf16(h)
            bits = pltpu.bitcast(h_hi.astype(F32), jnp.uint32)
            h_ref[rs, :] = bits[:, :D_MODEL // 2] | (bits[:, D_MODEL // 2:] >> 16)
            both = jnp.dot(h_hi, w_both, preferred_element_type=F32)
            logit_pieces.append(both[:, :ROUTE_LANES] + both[:, ROUTE_LANES:]
                                + jnp.dot(h_lo, w_hi, preferred_element_type=F32))
    if routed:
        logits = jnp.concatenate(logit_pieces, axis=0) + br_ref[...]
        lane = lax.broadcasted_iota(jnp.int32, logits.shape, 1).astype(F32)
        logits = jnp.where(lane < N_EXPERTS, logits, -jnp.inf)
        m1 = jnp.max(logits, axis=-1, keepdims=True)
        i1 = jnp.min(jnp.where(logits == m1, lane, float(ROUTE_LANES)), axis=-1, keepdims=True)
        rest = jnp.where(lane == i1, -jnp.inf, logits)
        m2 = jnp.max(rest, axis=-1, keepdims=True)
        i2 = jnp.min(jnp.where(rest == m2, lane, float(ROUTE_LANES)), axis=-1, keepdims=True)
        e2 = jnp.exp(m2 - m1)
        g1 = 1.0 / (1.0 + e2)
        g2 = e2 * g1

        @pl.when(pl.program_id(0) == 0)
        def _():
            carry_ref[...] = jnp.zeros_like(carry_ref)

        hit1 = lane == i1
        hit2 = lane == i2
        picked = jnp.where(hit1, 1.0, jnp.where(hit2, 1.0, 0.0))
        before = carry_ref[...] + jnp.dot(tri_ref[...], picked.astype(BF16), preferred_element_type=F32)
        rank1 = jnp.sum(jnp.where(hit1, before, 0.0), axis=-1, keepdims=True)
        rank2 = jnp.sum(jnp.where(hit2, before, 0.0), axis=-1, keepdims=True)
        total = carry_ref[...] + jnp.sum(picked, axis=0, keepdims=True)
        carry_ref[...] = total
        count_ref[...] = total
        cols = (i1, i2, g1, g2, rank1, rank2)
        route = jnp.zeros_like(logits)
        for c, val in enumerate(cols):
            route = jnp.where(lane == c, val, route)
        route_ref[...] = route


def _outproj(a, cm, x, mods, gpost, gpre, w_out, layer, router):
    n = x.shape[0]
    tm = TOKEN_TILE
    tiles_per_mod = (n // mods.shape[0]) // tm
    row = lambda i: (i, 0)
    const = lambda i: (0, 0)
    in_specs = [
        pl.BlockSpec((tm, D_ATTN), row),
        pl.BlockSpec((tm, D_CONV), row),
        pl.BlockSpec((tm, D_MODEL), row),
        pl.BlockSpec((None, 6, D_MODEL), lambda i: (i // tiles_per_mod, 0, 0)),
        pl.BlockSpec((1, D_MODEL), const),
        pl.BlockSpec((1, D_MODEL), const),
        pl.BlockSpec((None, D_MODEL, D_MODEL), lambda i: (layer, 0, 0)),
    ]
    args = [a, cm, x, mods, gpost, gpre, w_out]
    out_specs = [pl.BlockSpec((tm, D_MODEL), row), pl.BlockSpec((tm, D_MODEL), row)]
    out_shape = [jax.ShapeDtypeStruct((n, D_MODEL), F32), jax.ShapeDtypeStruct((n, D_MODEL), BF16)]
    scratch = []
    if router is not None:
        in_specs += [pl.BlockSpec((D_MODEL, ROUTE_LANES), const), pl.BlockSpec((1, ROUTE_LANES), const),
                     pl.BlockSpec((tm, tm), const)]
        args += list(router)
        out_specs[1] = pl.BlockSpec((tm, D_MODEL // 2), row)
        out_shape[1] = jax.ShapeDtypeStruct((n, D_MODEL // 2), jnp.uint32)
        out_specs += [pl.BlockSpec((tm, ROUTE_LANES), row), pl.BlockSpec((1, ROUTE_LANES), const)]
        out_shape += [jax.ShapeDtypeStruct((n, ROUTE_LANES), F32),
                      jax.ShapeDtypeStruct((1, ROUTE_LANES), F32)]
        scratch = [pltpu.VMEM((1, ROUTE_LANES), F32)]
    return pl.pallas_call(
        functools.partial(_outproj_kernel, routed=router is not None),
        grid=(n // tm,),
        in_specs=in_specs,
        out_specs=out_specs,
        out_shape=out_shape,
        scratch_shapes=scratch,
        compiler_params=_params("arbitrary"),
        name="outproj",
    )(*args)


def _swiglu_hidden(h, wg_ref, wu_ref, act_ref, up_col, width):
    for c in range(0, width, FF_CHUNK):
        g = jnp.dot(h, wg_ref[:, c:c + FF_CHUNK], preferred_element_type=F32)
        u = jnp.dot(h, wu_ref[:, up_col + c:up_col + c + FF_CHUNK], preferred_element_type=F32)
        act_ref[:, c:c + FF_CHUNK] = (_silu(g) * u).astype(BF16)


def _ffn_kernel(h_ref, x_ref, mod_ref, g_ref, wgu_ref, wd_ref, o_ref, act_ref):
    _swiglu_hidden(h_ref[...], wgu_ref, wgu_ref, act_ref, D_FF, D_FF)
    f = jnp.dot(act_ref[...], wd_ref[...], preferred_element_type=F32)
    o_ref[...] = x_ref[...] + mod_ref[5:6, :] * _rms(f, g_ref[...])


def _ffn_dense(h, x, mods, g, w_gu, w_down, index):
    n = x.shape[0]
    tm = FFN_TILE
    tiles_per_mod = (n // mods.shape[0]) // tm
    row = lambda i: (i, 0)
    whole = lambda i: (index, 0, 0)
    resident = pl.Buffered(1)
    return pl.pallas_call(
        _ffn_kernel,
        grid=(n // tm,),
        in_specs=[
            pl.BlockSpec((tm, D_MODEL), row),
            pl.BlockSpec((tm, D_MODEL), row),
            pl.BlockSpec((None, 6, D_MODEL), lambda i: (i // tiles_per_mod, 0, 0)),
            pl.BlockSpec((1, D_MODEL), lambda i: (0, 0)),
            pl.BlockSpec((None, D_MODEL, 2 * D_FF), whole, pipeline_mode=resident),
            pl.BlockSpec((None, D_FF, D_MODEL), whole, pipeline_mode=resident),
        ],
        out_specs=pl.BlockSpec((tm, D_MODEL), row),
        out_shape=jax.ShapeDtypeStruct((n, D_MODEL), F32),
        scratch_shapes=[pltpu.VMEM((tm, D_FF), BF16)],
        compiler_params=_params("arbitrary"),
        name="ffn_dense",
    )(h, x, mods, g, w_gu, w_down)


def _moe_kernel(te_ref, na_ref, h_ref, wg_ref, wu_ref, wd_ref, o_ref, act_ref, acc_ref):
    i = pl.program_id(0)
    j = pl.program_id(1)

    @pl.when(i >= na_ref[0])
    def _():
        o_ref[...] = jnp.zeros_like(o_ref)

    @pl.when(i < na_ref[0])
    def _():
        words = h_ref[...]
        h = jnp.concatenate(
            [pltpu.bitcast(words & jnp.uint32(0xFFFF0000), F32).astype(BF16),
             pltpu.bitcast(words << 16, F32).astype(BF16)], axis=1)
        _swiglu_hidden(h, wg_ref, wu_ref, act_ref, 0, MOE_CHUNK)
        part = jnp.dot(act_ref[...], wd_ref[...], preferred_element_type=F32)

        @pl.when(j == 0)
        def _():
            acc_ref[...] = part

        @pl.when((j > 0) & (j < N_MOE_CHUNKS - 1))
        def _():
            acc_ref[...] += part

        @pl.when(j == N_MOE_CHUNKS - 1)
        def _():
            o_ref[...] = (acc_ref[...] + part).astype(BF16)


def _moe_experts(hs, tile_expert, n_active, w_gu, w_down, index):
    rows = hs.shape[0]
    tm = MOE_TILE

    def tile(i, na):
        return jnp.minimum(i, na[0] - 1)

    def chunk(i, j, na):
        return jnp.where(i < na[0], j, N_MOE_CHUNKS - 1)

    grid_spec = pltpu.PrefetchScalarGridSpec(
        num_scalar_prefetch=2,
        grid=(rows // tm, N_MOE_CHUNKS),
        in_specs=[
            pl.BlockSpec((tm, D_MODEL // 2), lambda i, j, te, na: (tile(i, na), 0)),
            pl.BlockSpec((None, None, D_MODEL, MOE_CHUNK),
                         lambda i, j, te, na: (index, te[tile(i, na)], 0, chunk(i, j, na))),
            pl.BlockSpec((None, None, D_MODEL, MOE_CHUNK),
                         lambda i, j, te, na: (index, te[tile(i, na)], 0, N_MOE_CHUNKS + chunk(i, j, na))),
            pl.BlockSpec((None, None, MOE_CHUNK, D_MODEL),
                         lambda i, j, te, na: (index, te[tile(i, na)], chunk(i, j, na), 0)),
        ],
        out_specs=pl.BlockSpec((tm, D_MODEL), lambda i, j, te, na: (i, 0)),
        scratch_shapes=[pltpu.VMEM((tm, MOE_CHUNK), BF16), pltpu.VMEM((tm, D_MODEL), F32)],
    )
    return pl.pallas_call(
        _moe_kernel,
        grid_spec=grid_spec,
        out_shape=jax.ShapeDtypeStruct((rows, D_MODEL), BF16),
        compiler_params=_params("arbitrary", "arbitrary"),
        name="moe_experts",
    )(tile_expert, n_active, hs, w_gu, w_gu, w_down)


def _combine_kernel(y1_ref, y2_ref, route_ref, x_ref, mod_ref, g_ref, o_ref):
    f = (route_ref[:, 2:3] * y1_ref[...].astype(F32) + route_ref[:, 3:4] * y2_ref[...].astype(F32))
    o_ref[...] = x_ref[...] + mod_ref[5:6, :] * _rms(f, g_ref[...])


def _moe_combine(y1, y2, route, x, mods, g):
    n = x.shape[0]
    tm = TOKEN_TILE
    tiles_per_mod = (n // mods.shape[0]) // tm
    row = lambda i: (i, 0)
    return pl.pallas_call(
        _combine_kernel,
        grid=(n // tm,),
        in_specs=[
            pl.BlockSpec((tm, D_MODEL), row),
            pl.BlockSpec((tm, D_MODEL), row),
            pl.BlockSpec((tm, ROUTE_LANES), row),
            pl.BlockSpec((tm, D_MODEL), row),
            pl.BlockSpec((None, 6, D_MODEL), lambda i: (i // tiles_per_mod, 0, 0)),
            pl.BlockSpec((1, D_MODEL), lambda i: (0, 0)),
        ],
        out_specs=pl.BlockSpec((tm, D_MODEL), row),
        out_shape=jax.ShapeDtypeStruct((n, D_MODEL), F32),
        compiler_params=_params("arbitrary"),
        name="moe_combine",
    )(y1, y2, route, x, mods, g)


def _dispatch_kernel(pos_ref, h_ref, init_ref, o_ref, sem):
    del init_ref
    base = pl.program_id(0) * DISPATCH_TILE

    def issue(t, carry):
        src = h_ref.at[pl.ds(base + t, 1), :]
        pltpu.make_async_copy(src, o_ref.at[pl.ds(pos_ref[0, t], 1), :], sem).start()
        pltpu.make_async_copy(src, o_ref.at[pl.ds(pos_ref[1, t], 1), :], sem).start()
        return carry

    lax.fori_loop(0, DISPATCH_TILE, issue, 0)
    span = pl.ds(0, 2 * DISPATCH_TILE)
    pltpu.make_async_copy(h_ref.at[span, :], o_ref.at[span, :], sem).wait()


def _dispatch(pos, h_words, rows):
    n = h_words.shape[0]
    any_space = pl.BlockSpec(memory_space=pl.ANY)
    return pl.pallas_call(
        _dispatch_kernel,
        grid=(n // DISPATCH_TILE,),
        in_specs=[
            pl.BlockSpec((None, 2, DISPATCH_TILE), lambda i: (i, 0, 0), memory_space=pltpu.SMEM),
            any_space, any_space,
        ],
        out_specs=any_space,
        out_shape=jax.ShapeDtypeStruct((rows, D_MODEL // 2), jnp.uint32),
        scratch_shapes=[pltpu.SemaphoreType.DMA(())],
        input_output_aliases={2: 0},
        compiler_params=_params("arbitrary"),
        name="dispatch",
    )(pos, h_words, jnp.zeros((rows, D_MODEL // 2), jnp.uint32))


def _moe(h_words, x, route, counts, mods, g, w_gu, w_down, index):
    n = x.shape[0]
    tm = MOE_TILE
    rows = 2 * n + N_EXPERTS * tm
    counts = counts[0, :N_EXPERTS].astype(jnp.int32)
    padded = ((counts + tm - 1) // tm) * tm
    ends = jnp.cumsum(padded)
    starts = ends - padded
    sel = route[:, 0:6].astype(jnp.int32)
    pos1 = _rows(starts, sel[:, 0]) + sel[:, 4]
    pos2 = _rows(starts, sel[:, 1]) + sel[:, 5]
    pos = jnp.stack([pos1.reshape(-1, DISPATCH_TILE), pos2.reshape(-1, DISPATCH_TILE)], axis=1)
    tile_start = jnp.arange(rows // tm, dtype=jnp.int32) * tm
    tile_expert = jnp.minimum(
        jnp.sum((tile_start[:, None] >= ends[None, :]).astype(jnp.int32), axis=1), N_EXPERTS - 1)
    n_active = (ends[-1:] // tm).astype(jnp.int32)
    ys = _moe_experts(_dispatch(pos, h_words, rows), tile_expert, n_active, w_gu, w_down, index)
    return _moe_combine(_rows(ys, pos1), _rows(ys, pos2), route, x, mods, g)


def kernel(x_prompt, x_sample, c, c_ctx, cache_k, cache_v, w_mod, b_mod, g_pre_mix, g_post_mix,
           g_pre_ffn, g_post_ffn, w_in, w_out, attn_sink, conv_w, conv_b, conv_norm_g, conv_norm_b,
           ffn_w_gu, ffn_w_down, moe_w_router, moe_b_router, moe_w_gu, moe_w_down):
    batch, seq, _ = x_prompt.shape
    dec_batch, dec_seq, _ = x_sample.shape
    past = cache_k.shape[2]

    cvec = jnp.zeros((MOD_ROWS, D_MODEL), F32).at[0].set(c_ctx).at[1:1 + dec_batch].set(c)
    mods = _modulations(cvec, w_mod, b_mod)

    w_in_b = jnp.concatenate([_pair_heads(w_in[:, :, :D_ATTN], 2), w_in[:, :, D_ATTN:]], axis=2).astype(BF16)
    w_out_b = jnp.concatenate([_pair_heads(w_out[:, :D_ATTN, :], 1), w_out[:, D_ATTN:, :]], axis=1).astype(BF16)
    ffn_gu_b = ffn_w_gu.astype(BF16)
    ffn_down_b = ffn_w_down.astype(BF16)
    moe_gu_b = _to_bf16(moe_w_gu)
    moe_down_b = _to_bf16(moe_w_down)
    w_router = jnp.zeros((DEPTH // 2, D_MODEL, ROUTE_LANES), F32).at[:, :, :N_EXPERTS].set(moe_w_router)
    b_router = jnp.zeros((DEPTH // 2, 1, ROUTE_LANES), F32).at[:, 0, :N_EXPERTS].set(moe_b_router)
    sink2 = attn_sink * LOG2E
    cache_k4 = cache_k.reshape(dec_batch, DEPTH, past, D_KV)
    cache_v4 = cache_v.reshape(dec_batch, DEPTH, past, D_KV)
    rope_tabs = _rope_tables(dec_seq)
    gavg = jnp.kron(jnp.eye(CONV_GROUPS, dtype=F32),
                    jnp.full((D_CONV // CONV_GROUPS,) * 2, CONV_GROUPS / D_CONV, F32)).astype(BF16)
    tri = jnp.tril(jnp.ones((TOKEN_TILE, TOKEN_TILE), F32), -1).astype(BF16)
    conv_args = (conv_w, conv_b[:, None, :], gavg, conv_norm_g[:, None, :], conv_norm_b[:, None, :])

    def layer(l, x, mods_l, latent):
        n_seq, seq_len = (dec_batch, dec_seq) if latent else (batch, seq)
        row = lambda v: v[l][None, :]
        q, k, v, z = _inproj(x, mods_l, row(g_pre_mix), w_in_b, l, rope_tabs if latent else None, seq_len)
        if latent:
            a, cm = _mixers_latent(q, k, v, z, cache_k4, cache_v4, sink2[l], conv_args, l, n_seq, seq_len)
        else:
            a, cm = _mixers_context(q, k, v, z, sink2[l], conv_args, l, n_seq, seq_len)
        if l % 2 == 0:
            x1, h = _outproj(a, cm, x, mods_l, row(g_post_mix), row(g_pre_ffn), w_out_b, l, None)
            x2 = _ffn_dense(h, x1, mods_l, row(g_post_ffn), ffn_gu_b, ffn_down_b, l // 2)
        else:
            x1, h, route, counts = _outproj(a, cm, x, mods_l, row(g_post_mix), row(g_pre_ffn), w_out_b, l,
                                            (w_router[l // 2], b_router[l // 2], tri))
            x2 = _moe(h, x1, route, counts, mods_l, row(g_post_ffn), moe_gu_b, moe_down_b, l // 2)
        return x2, k, v

    xp = x_prompt.reshape(batch * seq, D_MODEL)
    ks, vs = [], []
    for l in range(DEPTH):
        xp, k_l, v_l = layer(l, xp, mods[l, 0:1], False)
        ks.append(k_l.reshape(batch, seq, N_KV_HEADS, HEAD_DIM))
        vs.append(v_l.reshape(batch, seq, N_KV_HEADS, HEAD_DIM))

    xs = x_sample.reshape(dec_batch * dec_seq, D_MODEL)
    for l in range(DEPTH):
        xs, _, _ = layer(l, xs, mods[l, 1:1 + dec_batch], True)

    return (xp.reshape(batch, seq, D_MODEL), xs.reshape(dec_batch, dec_seq, D_MODEL),
            jnp.stack(ks, axis=1), jnp.stack(vs, axis=1))
```

```python
import functools
import math

import jax
import jax.numpy as jnp
from jax import lax
from jax.experimental import pallas as pl
from jax.experimental.pallas import tpu as pltpu

F32 = jnp.float32
BF16 = jnp.bfloat16

D_MODEL = 1024
DEPTH = 4
GRID_W = 64
D_ATTN = 512
D_CONV = 512
HEAD_DIM = 64
N_HEADS = 8
N_KV_HEADS = 2
N_REP = N_HEADS // N_KV_HEADS
D_KV = N_KV_HEADS * HEAD_DIM
D_IN = D_ATTN + 2 * D_KV + 2 * D_CONV
WINDOW = 128
BLOCK = 128
ATTN_SCALE = HEAD_DIM ** -0.5
ROPE_BASE = 10000.0
CONV_WIDTH = 31
CONV_HALF = CONV_WIDTH // 2
CONV_GROUPS = 8
D_FF = 2816
N_EXPERTS = 8
D_FF_EXPERT = 3584
EPS = 1e-6
NEG_INF = -1e30
LOG2E = math.log2(math.e)

LANES = 128
SUBLANES = 8
VMEM_LIMIT = 48 * 1024 * 1024

TOKEN_TILE = 1024
ROW_PIECE = 512
FFN_TILE = 512
ATTN_BLOCKS = 4
CONV_HALO = 16
CONV_ROWS = 128
CONV_SPAN = SUBLANES * ((CONV_WIDTH - 1) // SUBLANES)
FF_CHUNK = 256
MOE_TILE = 512
MOE_CHUNK = D_FF_EXPERT // 2
N_MOE_CHUNKS = D_FF_EXPERT // MOE_CHUNK
MOD_ROWS = 16
ROUTE_LANES = 128
CAST_BLOCK_BYTES = 8 * 1024 * 1024
CAST_COLS = 1024

assert N_KV_HEADS * HEAD_DIM == LANES and D_ATTN == N_REP * LANES
assert WINDOW == BLOCK


def _params(*sem):
    return pltpu.CompilerParams(dimension_semantics=sem, vmem_limit_bytes=VMEM_LIMIT)


def _silu(x):
    return x * jax.nn.sigmoid(x)


def _rms(x, g):
    return x * lax.rsqrt(jnp.mean(x * x, axis=-1, keepdims=True) + EPS) * g


def _split_bf16(x):
    hi = x.astype(BF16)
    lo = (x - hi.astype(F32)).astype(BF16)
    return hi, lo


def _rows(arr, idx):
    return arr.at[idx].get(mode="promise_in_bounds")


def _cast_kernel(x_ref, o_ref):
    o_ref[...] = x_ref[...].astype(BF16)


def _to_bf16(w, index):
    shape = w.shape[1:]
    cols = shape[-1]
    w2 = w.reshape(-1, cols)
    rows = w2.shape[0] // w.shape[0]
    block = CAST_BLOCK_BYTES // (4 * CAST_COLS)
    assert rows % block == 0 and cols % CAST_COLS == 0
    first = index * (rows // block)
    out = pl.pallas_call(
        _cast_kernel,
        grid=(rows // block, cols // CAST_COLS),
        in_specs=[pl.BlockSpec((block, CAST_COLS), lambda i, j: (first + i, j))],
        out_specs=pl.BlockSpec((block, CAST_COLS), lambda i, j: (i, j)),
        out_shape=jax.ShapeDtypeStruct((rows, cols), BF16),
        compiler_params=_params("arbitrary", "arbitrary"),
        name="weight_cast",
    )(w2)
    return out.reshape(shape)


def _mod_kernel(c_ref, w_ref, b_ref, o_ref):
    a = _silu(c_ref[...]).astype(BF16)
    o_ref[...] = jnp.dot(a, w_ref[...].astype(BF16), preferred_element_type=F32) + b_ref[...]


def _modulations(cvec, w_mod, b_mod):
    out = pl.pallas_call(
        _mod_kernel,
        grid=(DEPTH, 6),
        in_specs=[
            pl.BlockSpec((MOD_ROWS, D_MODEL), lambda l, s: (0, 0)),
            pl.BlockSpec((None, D_MODEL, D_MODEL), lambda l, s: (l, 0, s)),
            pl.BlockSpec((None, 1, D_MODEL), lambda l, s: (l, 0, s)),
        ],
        out_specs=pl.BlockSpec((None, MOD_ROWS, D_MODEL), lambda l, s: (l, 0, s)),
        out_shape=jax.ShapeDtypeStruct((DEPTH, MOD_ROWS, 6 * D_MODEL), F32),
        compiler_params=_params("arbitrary", "arbitrary"),
        name="modulations",
    )(cvec, w_mod, b_mod.reshape(DEPTH, 1, 6 * D_MODEL))
    return out.reshape(DEPTH, MOD_ROWS, 6, D_MODEL)


def _rope(x, cos, sin):
    lane = lax.broadcasted_iota(jnp.int32, x.shape, 1)
    width = x.shape[1]
    partner = jnp.where((lane & 16) == 0, pltpu.roll(x, width - 16, 1), pltpu.roll(x, 16, 1))
    return x * cos + partner * sin


def _inproj_kernel(*refs, rope):
    if rope:
        x_ref, mod_ref, g_ref, w_ref, cos_ref, sin_ref, q_ref, k_ref, v_ref, z_ref = refs
    else:
        x_ref, mod_ref, g_ref, w_ref, q_ref, k_ref, v_ref, z_ref = refs
    for r0 in range(0, x_ref.shape[0], ROW_PIECE):
        rs = slice(r0, r0 + ROW_PIECE)
        h = _rms(x_ref[rs, :], g_ref[...]) * (1.0 + mod_ref[1:2, :]) + mod_ref[0:1, :]
        p = jnp.dot(h.astype(BF16), w_ref[...], preferred_element_type=F32)
        q = p[:, :D_ATTN]
        k = p[:, D_ATTN:D_ATTN + D_KV]
        if rope:
            cos = cos_ref[rs, :]
            sin = sin_ref[rs, :]
            q = _rope(q, jnp.concatenate([cos] * N_REP, axis=1), jnp.concatenate([sin] * N_REP, axis=1))
            k = _rope(k, cos, sin)
        q_ref[rs, :] = (q * (ATTN_SCALE * LOG2E)).astype(BF16)
        k_ref[rs, :] = k
        v_ref[rs, :] = p[:, D_ATTN + D_KV:D_ATTN + 2 * D_KV]
        a = p[:, D_ATTN + 2 * D_KV:D_ATTN + 2 * D_KV + D_CONV]
        gt = p[:, D_ATTN + 2 * D_KV + D_CONV:]
        z_ref[rs, :] = (a * jax.nn.sigmoid(gt)).astype(BF16)


def _inproj(x, mods, g, w_in, layer, rope_tabs, seq_len):
    n = x.shape[0]
    tm = TOKEN_TILE
    tiles_per_mod = (n // mods.shape[0]) // tm
    row = lambda i: (i, 0)
    const = lambda i: (0, 0)
    in_specs = [
        pl.BlockSpec((tm, D_MODEL), row),
        pl.BlockSpec((None, 6, D_MODEL), lambda i: (i // tiles_per_mod, 0, 0)),
        pl.BlockSpec((1, D_MODEL), const),
        pl.BlockSpec((None, D_MODEL, D_IN), lambda i: (layer, 0, 0)),
    ]
    args = [x, mods, g, w_in]
    if rope_tabs is not None:
        tiles_per_seq = seq_len // tm
        tab = pl.BlockSpec((tm, LANES), lambda i: (i % tiles_per_seq, 0))
        in_specs += [tab, tab]
        args += list(rope_tabs)
    return pl.pallas_call(
        functools.partial(_inproj_kernel, rope=rope_tabs is not None),
        grid=(n // tm,),
        in_specs=in_specs,
        out_specs=[
            pl.BlockSpec((tm, D_ATTN), row),
            pl.BlockSpec((tm, D_KV), row),
            pl.BlockSpec((tm, D_KV), row),
            pl.BlockSpec((tm, D_CONV), row),
        ],
        out_shape=[
            jax.ShapeDtypeStruct((n, D_ATTN), BF16),
            jax.ShapeDtypeStruct((n, D_KV), F32),
            jax.ShapeDtypeStruct((n, D_KV), F32),
            jax.ShapeDtypeStruct((n, D_CONV), BF16),
        ],
        compiler_params=_params("arbitrary"),
        name="inproj",
    )(*args)


def _rope_tables(seq_len):
    half = HEAD_DIM // 2
    n_freq = half // 2
    t = jnp.arange(seq_len)
    inv = ROPE_BASE ** (-jnp.arange(n_freq, dtype=F32) * 2.0 / half)
    ang_r = (t // GRID_W).astype(F32)[:, None] * inv[None, :]
    ang_c = (t % GRID_W).astype(F32)[:, None] * inv[None, :]
    cos = jnp.concatenate([jnp.cos(ang_r)] * 2 + [jnp.cos(ang_c)] * 2, axis=-1)
    sin = jnp.concatenate([-jnp.sin(ang_r), jnp.sin(ang_r), -jnp.sin(ang_c), jnp.sin(ang_c)], axis=-1)
    return jnp.tile(cos, (1, LANES // HEAD_DIM)), jnp.tile(sin, (1, LANES // HEAD_DIM))


def _pair_heads(w, axis):
    shape = w.shape
    split = shape[:axis] + (N_KV_HEADS, N_REP, HEAD_DIM) + shape[axis + 1:]
    return jnp.swapaxes(w.reshape(split), axis, axis + 1).reshape(shape)


def _attend(q, sink_ref, ks, vs, prev_ok=None, next_ok=None):
    qb = q.shape[0]
    past = ks[0].shape[0]
    windowed = len(ks) > 1
    low = lax.broadcasted_iota(jnp.int32, (qb, LANES), 1) < HEAD_DIM
    zero = jnp.zeros((qb, LANES), BF16)
    blocks, sinks = [], []
    for g in range(N_REP):
        slab = q[:, g * LANES:(g + 1) * LANES]
        blocks += [jnp.where(low, slab, zero), jnp.where(low, zero, slab)]
        sinks += [jnp.full((qb, 1), sink_ref[g], F32), jnp.full((qb, 1), sink_ref[N_REP + g], F32)]
    qz = jnp.concatenate(blocks, axis=0)
    sink = jnp.concatenate(sinks, axis=0)
    k_all = jnp.concatenate(ks, axis=0).astype(BF16)
    v_all = jnp.concatenate(vs, axis=0).astype(BF16)
    v_ext = jnp.concatenate([v_all, jnp.ones_like(v_all)], axis=1)
    s = lax.dot_general(qz, k_all, (((1,), (1,)), ((), ())), preferred_element_type=F32)
    pieces = [s[:, c:c + LANES] for c in range(0, past, LANES)]
    if windowed:
        qi = lax.broadcasted_iota(jnp.int32, (qb, LANES), 0)
        kk = lax.broadcasted_iota(jnp.int32, (qb, LANES), 1)
        bias_prev = jnp.where((kk >= qi) & prev_ok, 0.0, NEG_INF)
        bias_next = jnp.where((kk <= qi) & next_ok, 0.0, NEG_INF)
        rep = lambda b: jnp.concatenate([b] * N_HEADS, axis=0)
        pieces += [s[:, past:past + BLOCK] + rep(bias_prev),
                   s[:, past + BLOCK:past + 2 * BLOCK],
                   s[:, past + 2 * BLOCK:] + rep(bias_next)]
    m = jnp.maximum(jnp.max(functools.reduce(jnp.maximum, pieces), axis=-1, keepdims=True), sink)
    p = jnp.concatenate([jnp.exp2(x - m).astype(BF16) for x in pieces], axis=1)
    o = jnp.dot(p, v_ext, preferred_element_type=F32)
    r = o[:, :LANES] / (o[:, LANES:] + jnp.exp2(sink - m))
    slabs = [jnp.where(low, r[2 * g * qb:(2 * g + 1) * qb], r[(2 * g + 1) * qb:(2 * g + 2) * qb])
             for g in range(N_REP)]
    return jnp.concatenate(slabs, axis=1).astype(BF16)


def _conv(z, halo_prev, halo_next, w_ref, b_ref, gavg_ref, gn_ref, bn_ref, ext_ref, sh_ref, y_ref):
    tt = z.shape[0]
    ext_ref[0:CONV_HALO, :] = halo_prev
    ext_ref[CONV_HALO:CONV_HALO + tt, :] = z
    ext_ref[CONV_HALO + tt:, :] = halo_next
    base = CONV_HALO - CONV_HALF
    for r in range(SUBLANES):
        sh_ref[r] = ext_ref[base + r:base + r + tt + CONV_SPAN, :]
    for c in range(D_CONV // LANES):
        cs = slice(c * LANES, (c + 1) * LANES)
        w = w_ref[:, cs]
        for rb in range(tt // CONV_ROWS):
            r0 = rb * CONV_ROWS
            acc = jnp.broadcast_to(b_ref[:, cs], (CONV_ROWS, LANES))
            for j in range(CONV_WIDTH):
                r, off = j % SUBLANES, r0 + SUBLANES * (j // SUBLANES)
                acc = acc + w[j:j + 1, :] * sh_ref[r, off:off + CONV_ROWS, cs]
            y_ref[r0:r0 + CONV_ROWS, cs] = acc
    y = y_ref[...]
    gavg = gavg_ref[...]

    def group_mean(a):
        hi, lo = _split_bf16(a)
        return (jnp.dot(hi, gavg, preferred_element_type=F32)
                + jnp.dot(lo, gavg, preferred_element_type=F32))

    d = y - group_mean(y)
    var = group_mean(d * d)
    zn = d * lax.rsqrt(var + EPS) * gn_ref[...] + bn_ref[...]
    return _silu(zn).astype(BF16)


def _mixer_context_kernel(sink_ref, q_ref, k_ref, v_ref, z_ref, w_ref, b_ref, gavg_ref, gn_ref, bn_ref,
                          a_ref, cm_ref, ext_ref, sh_ref, y_ref):
    a_ref[...] = _attend(q_ref[...], sink_ref, [k_ref[...]], [v_ref[...]])
    edge = jnp.zeros((CONV_HALO, D_CONV), F32)
    cm_ref[...] = _conv(z_ref[...].astype(F32), edge, edge, w_ref, b_ref, gavg_ref, gn_ref, bn_ref,
                        ext_ref, sh_ref, y_ref)


def _mixer_latent_kernel(sink_ref, q_ref, ck_ref, cv_ref, kp_ref, kc_ref, kn_ref, vp_ref, vc_ref, vn_ref,
                         z_ref, zp_ref, zn_ref, w_ref, b_ref, gavg_ref, gn_ref, bn_ref,
                         a_ref, cm_ref, ext_ref, sh_ref, y_ref, *, n_blocks):
    j = pl.program_id(1)
    k_blocks = [kp_ref[...]] + [kc_ref[s * BLOCK:(s + 1) * BLOCK, :] for s in range(ATTN_BLOCKS)] + [kn_ref[...]]
    v_blocks = [vp_ref[...]] + [vc_ref[s * BLOCK:(s + 1) * BLOCK, :] for s in range(ATTN_BLOCKS)] + [vn_ref[...]]
    ck = ck_ref[...]
    cv = cv_ref[...]
    for s in range(ATTN_BLOCKS):
        block = j * ATTN_BLOCKS + s
        a_ref[s * BLOCK:(s + 1) * BLOCK, :] = _attend(
            q_ref[s * BLOCK:(s + 1) * BLOCK, :], sink_ref,
            [ck] + k_blocks[s:s + 3], [cv] + v_blocks[s:s + 3],
            prev_ok=block > 0, next_ok=block < n_blocks - 1)
    first = j == 0
    last = j == n_blocks // ATTN_BLOCKS - 1
    cm_ref[...] = _conv(z_ref[...].astype(F32),
                        jnp.where(first, 0.0, zp_ref[...].astype(F32)),
                        jnp.where(last, 0.0, zn_ref[...].astype(F32)),
                        w_ref, b_ref, gavg_ref, gn_ref, bn_ref, ext_ref, sh_ref, y_ref)


def _conv_specs(layer, ngrid):
    const = (lambda b: (0, 0)) if ngrid == 1 else (lambda b, j: (0, 0))
    per_layer = (lambda b: (layer, 0, 0)) if ngrid == 1 else (lambda b, j: (layer, 0, 0))
    return [
        pl.BlockSpec((None, CONV_WIDTH, D_CONV), per_layer),
        pl.BlockSpec((None, 1, D_CONV), per_layer),
        pl.BlockSpec((D_CONV, D_CONV), const),
        pl.BlockSpec((None, 1, D_CONV), per_layer),
        pl.BlockSpec((None, 1, D_CONV), per_layer),
    ]


def _conv_scratch(tt):
    return [
        pltpu.VMEM((tt + 2 * CONV_HALO, D_CONV), F32),
        pltpu.VMEM((SUBLANES, tt + CONV_SPAN, D_CONV), F32),
        pltpu.VMEM((tt, D_CONV), F32),
    ]


def _mixers_latent(q, k, v, z, cache_k, cache_v, sink, conv_args, layer, batch, seq_len):
    nb = seq_len // BLOCK
    tt = ATTN_BLOCKS * BLOCK
    q3 = q.reshape(batch, seq_len, D_ATTN)
    k3 = k.reshape(batch, seq_len, D_KV)
    v3 = v.reshape(batch, seq_len, D_KV)
    z3 = z.reshape(batch, seq_len, D_CONV)
    past = cache_k.shape[2]
    halo_per_tile = tt // CONV_HALO
    n_halo = seq_len // CONV_HALO
    blk = lambda f: pl.BlockSpec((None, BLOCK, D_KV), f)
    prev_map = lambda b, j: (b, jnp.maximum(j * ATTN_BLOCKS - 1, 0), 0)
    cur_map = lambda b, j: (b, j, 0)
    next_map = lambda b, j: (b, jnp.minimum((j + 1) * ATTN_BLOCKS, nb - 1), 0)
    cur = pl.BlockSpec((None, tt, D_KV), cur_map)
    cache_spec = pl.BlockSpec((None, None, past, D_KV), lambda b, j: (b, layer, 0, 0))
    a, cm = pl.pallas_call(
        functools.partial(_mixer_latent_kernel, n_blocks=nb),
        grid=(batch, nb // ATTN_BLOCKS),
        in_specs=[
            pl.BlockSpec(memory_space=pltpu.SMEM),
            pl.BlockSpec((None, tt, D_ATTN), cur_map),
            cache_spec, cache_spec,
            blk(prev_map), cur, blk(next_map),
            blk(prev_map), cur, blk(next_map),
            pl.BlockSpec((None, tt, D_CONV), cur_map),
            pl.BlockSpec((None, CONV_HALO, D_CONV),
                         lambda b, j: (b, jnp.maximum(j * halo_per_tile - 1, 0), 0)),
            pl.BlockSpec((None, CONV_HALO, D_CONV),
                         lambda b, j: (b, jnp.minimum((j + 1) * halo_per_tile, n_halo - 1), 0)),
        ] + _conv_specs(layer, 2),
        out_specs=[pl.BlockSpec((None, tt, D_ATTN), cur_map), pl.BlockSpec((None, tt, D_CONV), cur_map)],
        out_shape=[jax.ShapeDtypeStruct((batch, seq_len, D_ATTN), BF16),
                   jax.ShapeDtypeStruct((batch, seq_len, D_CONV), BF16)],
        scratch_shapes=_conv_scratch(tt),
        compiler_params=_params("arbitrary", "arbitrary"),
        name="mixer_latent",
    )(sink, q3, cache_k, cache_v, k3, k3, k3, v3, v3, v3, z3, z3, z3, *conv_args)
    return a.reshape(batch * seq_len, D_ATTN), cm.reshape(batch * seq_len, D_CONV)


def _mixers_context(q, k, v, z, sink, conv_args, layer, batch, seq_len):
    q3 = q.reshape(batch, seq_len, D_ATTN)
    k3 = k.reshape(batch, seq_len, D_KV)
    v3 = v.reshape(batch, seq_len, D_KV)
    z3 = z.reshape(batch, seq_len, D_CONV)
    seq = lambda b: (b, 0, 0)
    a, cm = pl.pallas_call(
        _mixer_context_kernel,
        grid=(batch,),
        in_specs=[
            pl.BlockSpec(memory_space=pltpu.SMEM),
            pl.BlockSpec((None, seq_len, D_ATTN), seq),
            pl.BlockSpec((None, seq_len, D_KV), seq),
            pl.BlockSpec((None, seq_len, D_KV), seq),
            pl.BlockSpec((None, seq_len, D_CONV), seq),
        ] + _conv_specs(layer, 1),
        out_specs=[pl.BlockSpec((None, seq_len, D_ATTN), seq), pl.BlockSpec((None, seq_len, D_CONV), seq)],
        out_shape=[jax.ShapeDtypeStruct((batch, seq_len, D_ATTN), BF16),
                   jax.ShapeDtypeStruct((batch, seq_len, D_CONV), BF16)],
        scratch_shapes=_conv_scratch(seq_len),
        compiler_params=_params("arbitrary"),
        name="mixer_context",
    )(sink, q3, k3, v3, z3, *conv_args)
    return a.reshape(batch * seq_len, D_ATTN), cm.reshape(batch * seq_len, D_CONV)


def _outproj_kernel(*refs, routed):
    if routed:
        (a_ref, cm_ref, x_ref, mod_ref, gpost_ref, gpre_ref, w_ref, wr_ref, br_ref, tri_ref,
         x1_ref, h_ref, route_ref, count_ref, carry_ref) = refs
    else:
        a_ref, cm_ref, x_ref, mod_ref, gpost_ref, gpre_ref, w_ref, x1_ref, h_ref = refs
    if routed:
        w_hi, w_lo = _split_bf16(wr_ref[...])
        w_both = jnp.concatenate([w_hi, w_lo], axis=1)
    logit_pieces = []
    for r0 in range(0, x_ref.shape[0], ROW_PIECE):
        rs = slice(r0, r0 + ROW_PIECE)
        m = (jnp.dot(a_ref[rs, :], w_ref[:D_ATTN, :], preferred_element_type=F32)
             + jnp.dot(cm_ref[rs, :], w_ref[D_ATTN:, :], preferred_element_type=F32))
        x1 = x_ref[rs, :] + mod_ref[2:3, :] * _rms(m, gpost_ref[...])
        x1_ref[rs, :] = x1
        h = _rms(x1, gpre_ref[...]) * (1.0 + mod_ref[4:5, :]) + mod_ref[3:4, :]
        h_ref[rs, :] = h.astype(BF16)
        if routed:
            h_hi, h_lo = _split_bf16(h)
            both = jnp.dot(h_hi, w_both, preferred_element_type=F32)
            logit_pieces.append(both[:, :ROUTE_LANES] + both[:, ROUTE_LANES:]
                                + jnp.dot(h_lo, w_hi, preferred_element_type=F32))
    if routed:
        logits = jnp.concatenate(logit_pieces, axis=0) + br_ref[...]
        lane = lax.broadcasted_iota(jnp.int32, logits.shape, 1).astype(F32)
        logits = jnp.where(lane < N_EXPERTS, logits, -jnp.inf)
        m1 = jnp.max(logits, axis=-1, keepdims=True)
        i1 = jnp.min(jnp.where(logits == m1, lane, float(ROUTE_LANES)), axis=-1, keepdims=True)
        rest = jnp.where(lane == i1, -jnp.inf, logits)
        m2 = jnp.max(rest, axis=-1, keepdims=True)
        i2 = jnp.min(jnp.where(rest == m2, lane, float(ROUTE_LANES)), axis=-1, keepdims=True)
        e2 = jnp.exp(m2 - m1)
        g1 = 1.0 / (1.0 + e2)
        g2 = e2 * g1

        @pl.when(pl.program_id(0) == 0)
        def _():
            carry_ref[...] = jnp.zeros_like(carry_ref)

        hit1 = lane == i1
        hit2 = lane == i2
        picked = jnp.where(hit1, 1.0, jnp.where(hit2, 1.0, 0.0))
        before = carry_ref[...] + jnp.dot(tri_ref[...], picked.astype(BF16), preferred_element_type=F32)
        rank1 = jnp.sum(jnp.where(hit1, before, 0.0), axis=-1, keepdims=True)
        rank2 = jnp.sum(jnp.where(hit2, before, 0.0), axis=-1, keepdims=True)
        total = carry_ref[...] + jnp.sum(picked, axis=0, keepdims=True)
        carry_ref[...] = total
        count_ref[...] = total
        cols = (i1, i2, g1, g2, rank1, rank2)
        route = jnp.zeros_like(logits)
        for c, val in enumerate(cols):
            route = jnp.where(lane == c, val, route)
        route_ref[...] = route


def _outproj(a, cm, x, mods, gpost, gpre, w_out, layer, router):
    n = x.shape[0]
    tm = TOKEN_TILE
    tiles_per_mod = (n // mods.shape[0]) // tm
    row = lambda i: (i, 0)
    const = lambda i: (0, 0)
    in_specs = [
        pl.BlockSpec((tm, D_ATTN), row),
        pl.BlockSpec((tm, D_CONV), row),
        pl.BlockSpec((tm, D_MODEL), row),
        pl.BlockSpec((None, 6, D_MODEL), lambda i: (i // tiles_per_mod, 0, 0)),
        pl.BlockSpec((1, D_MODEL), const),
        pl.BlockSpec((1, D_MODEL), const),
        pl.BlockSpec((None, D_MODEL, D_MODEL), lambda i: (layer, 0, 0)),
    ]
    args = [a, cm, x, mods, gpost, gpre, w_out]
    out_specs = [pl.BlockSpec((tm, D_MODEL), row), pl.BlockSpec((tm, D_MODEL), row)]
    out_shape = [jax.ShapeDtypeStruct((n, D_MODEL), F32), jax.ShapeDtypeStruct((n, D_MODEL), BF16)]
    scratch = []
    if router is not None:
        in_specs += [pl.BlockSpec((D_MODEL, ROUTE_LANES), const), pl.BlockSpec((1, ROUTE_LANES), const),
                     pl.BlockSpec((tm, tm), const)]
        args += list(router)
        out_specs += [pl.BlockSpec((tm, ROUTE_LANES), row), pl.BlockSpec((1, ROUTE_LANES), const)]
        out_shape += [jax.ShapeDtypeStruct((n, ROUTE_LANES), F32),
                      jax.ShapeDtypeStruct((1, ROUTE_LANES), F32)]
        scratch = [pltpu.VMEM((1, ROUTE_LANES), F32)]
    return pl.pallas_call(
        functools.partial(_outproj_kernel, routed=router is not None),
        grid=(n // tm,),
        in_specs=in_specs,
        out_specs=out_specs,
        out_shape=out_shape,
        scratch_shapes=scratch,
        compiler_params=_params("arbitrary"),
        name="outproj",
    )(*args)


def _swiglu_hidden(h, wg_ref, wu_ref, act_ref, up_col, width):
    for c in range(0, width, FF_CHUNK):
        g = jnp.dot(h, wg_ref[:, c:c + FF_CHUNK], preferred_element_type=F32)
        u = jnp.dot(h, wu_ref[:, up_col + c:up_col + c + FF_CHUNK], preferred_element_type=F32)
        act_ref[:, c:c + FF_CHUNK] = (_silu(g) * u).astype(BF16)


def _ffn_kernel(h_ref, x_ref, mod_ref, g_ref, wgu_ref, wd_ref, o_ref, act_ref):
    _swiglu_hidden(h_ref[...], wgu_ref, wgu_ref, act_ref, D_FF, D_FF)
    f = jnp.dot(act_ref[...], wd_ref[...], preferred_element_type=F32)
    o_ref[...] = x_ref[...] + mod_ref[5:6, :] * _rms(f, g_ref[...])


def _ffn_dense(h, x, mods, g, w_gu, w_down, index):
    n = x.shape[0]
    tm = FFN_TILE
    tiles_per_mod = (n // mods.shape[0]) // tm
    row = lambda i: (i, 0)
    whole = lambda i: (index, 0, 0)
    resident = pl.Buffered(1)
    return pl.pallas_call(
        _ffn_kernel,
        grid=(n // tm,),
        in_specs=[
            pl.BlockSpec((tm, D_MODEL), row),
            pl.BlockSpec((tm, D_MODEL), row),
            pl.BlockSpec((None, 6, D_MODEL), lambda i: (i // tiles_per_mod, 0, 0)),
            pl.BlockSpec((1, D_MODEL), lambda i: (0, 0)),
            pl.BlockSpec((None, D_MODEL, 2 * D_FF), whole, pipeline_mode=resident),
            pl.BlockSpec((None, D_FF, D_MODEL), whole, pipeline_mode=resident),
        ],
        out_specs=pl.BlockSpec((tm, D_MODEL), row),
        out_shape=jax.ShapeDtypeStruct((n, D_MODEL), F32),
        scratch_shapes=[pltpu.VMEM((tm, D_FF), BF16)],
        compiler_params=_params("arbitrary"),
        name="ffn_dense",
    )(h, x, mods, g, w_gu, w_down)


def _moe_kernel(te_ref, na_ref, h_ref, wg_ref, wu_ref, wd_ref, o_ref, act_ref, acc_ref):
    i = pl.program_id(0)
    j = pl.program_id(1)

    @pl.when(i >= na_ref[0])
    def _():
        o_ref[...] = jnp.zeros_like(o_ref)

    @pl.when(i < na_ref[0])
    def _():
        _swiglu_hidden(h_ref[...], wg_ref, wu_ref, act_ref, 0, MOE_CHUNK)
        part = jnp.dot(act_ref[...], wd_ref[...], preferred_element_type=F32)

        @pl.when(j == 0)
        def _():
            acc_ref[...] = part

        @pl.when((j > 0) & (j < N_MOE_CHUNKS - 1))
        def _():
            acc_ref[...] += part

        @pl.when(j == N_MOE_CHUNKS - 1)
        def _():
            o_ref[...] = (acc_ref[...] + part).astype(BF16)


def _moe_experts(hs, tile_expert, n_active, w_gu, w_down):
    rows = hs.shape[0]
    tm = MOE_TILE

    def tile(i, na):
        return jnp.minimum(i, na[0] - 1)

    def chunk(i, j, na):
        return jnp.where(i < na[0], j, N_MOE_CHUNKS - 1)

    grid_spec = pltpu.PrefetchScalarGridSpec(
        num_scalar_prefetch=2,
        grid=(rows // tm, N_MOE_CHUNKS),
        in_specs=[
            pl.BlockSpec((tm, D_MODEL), lambda i, j, te, na: (tile(i, na), 0)),
            pl.BlockSpec((None, D_MODEL, MOE_CHUNK),
                         lambda i, j, te, na: (te[tile(i, na)], 0, chunk(i, j, na))),
            pl.BlockSpec((None, D_MODEL, MOE_CHUNK),
                         lambda i, j, te, na: (te[tile(i, na)], 0, N_MOE_CHUNKS + chunk(i, j, na))),
            pl.BlockSpec((None, MOE_CHUNK, D_MODEL),
                         lambda i, j, te, na: (te[tile(i, na)], chunk(i, j, na), 0)),
        ],
        out_specs=pl.BlockSpec((tm, D_MODEL), lambda i, j, te, na: (i, 0)),
        scratch_shapes=[pltpu.VMEM((tm, MOE_CHUNK), BF16), pltpu.VMEM((tm, D_MODEL), F32)],
    )
    return pl.pallas_call(
        _moe_kernel,
        grid_spec=grid_spec,
        out_shape=jax.ShapeDtypeStruct((rows, D_MODEL), BF16),
        compiler_params=_params("arbitrary", "arbitrary"),
        name="moe_experts",
    )(tile_expert, n_active, hs, w_gu, w_gu, w_down)


def _combine_kernel(y1_ref, y2_ref, route_ref, x_ref, mod_ref, g_ref, o_ref):
    f = (route_ref[:, 2:3] * y1_ref[...].astype(F32) + route_ref[:, 3:4] * y2_ref[...].astype(F32))
    o_ref[...] = x_ref[...] + mod_ref[5:6, :] * _rms(f, g_ref[...])


def _moe_combine(y1, y2, route, x, mods, g):
    n = x.shape[0]
    tm = TOKEN_TILE
    tiles_per_mod = (n // mods.shape[0]) // tm
    row = lambda i: (i, 0)
    return pl.pallas_call(
        _combine_kernel,
        grid=(n // tm,),
        in_specs=[
            pl.BlockSpec((tm, D_MODEL), row),
            pl.BlockSpec((tm, D_MODEL), row),
            pl.BlockSpec((tm, ROUTE_LANES), row),
            pl.BlockSpec((tm, D_MODEL), row),
            pl.BlockSpec((None, 6, D_MODEL), lambda i: (i // tiles_per_mod, 0, 0)),
            pl.BlockSpec((1, D_MODEL), lambda i: (0, 0)),
        ],
        out_specs=pl.BlockSpec((tm, D_MODEL), row),
        out_shape=jax.ShapeDtypeStruct((n, D_MODEL), F32),
        compiler_params=_params("arbitrary"),
        name="moe_combine",
    )(y1, y2, route, x, mods, g)


def _moe(h, x, route, counts, mods, g, w_gu, w_down):
    n = x.shape[0]
    tm = MOE_TILE
    rows = 2 * n + N_EXPERTS * tm
    counts = counts[0, :N_EXPERTS].astype(jnp.int32)
    padded = ((counts + tm - 1) // tm) * tm
    ends = jnp.cumsum(padded)
    starts = ends - padded
    sel = route[:, 0:6].astype(jnp.int32)
    pos1 = _rows(starts, sel[:, 0]) + sel[:, 4]
    pos2 = _rows(starts, sel[:, 1]) + sel[:, 5]
    token = jnp.arange(n, dtype=jnp.int32)
    src = jnp.zeros((rows,), jnp.int32).at[jnp.concatenate([pos1, pos2])].set(
        jnp.concatenate([token, token]), mode="promise_in_bounds", unique_indices=True)
    tile_start = jnp.arange(rows // tm, dtype=jnp.int32) * tm
    tile_expert = jnp.minimum(
        jnp.sum((tile_start[:, None] >= ends[None, :]).astype(jnp.int32), axis=1), N_EXPERTS - 1)
    n_active = (ends[-1:] // tm).astype(jnp.int32)
    ys = _moe_experts(_rows(h, src), tile_expert, n_active, w_gu, w_down)
    return _moe_combine(_rows(ys, pos1), _rows(ys, pos2), route, x, mods, g)


def kernel(x_prompt, x_sample, c, c_ctx, cache_k, cache_v, w_mod, b_mod, g_pre_mix, g_post_mix,
           g_pre_ffn, g_post_ffn, w_in, w_out, attn_sink, conv_w, conv_b, conv_norm_g, conv_norm_b,
           ffn_w_gu, ffn_w_down, moe_w_router, moe_b_router, moe_w_gu, moe_w_down):
    batch, seq, _ = x_prompt.shape
    dec_batch, dec_seq, _ = x_sample.shape
    past = cache_k.shape[2]

    cvec = jnp.zeros((MOD_ROWS, D_MODEL), F32).at[0].set(c_ctx).at[1:1 + dec_batch].set(c)
    mods = _modulations(cvec, w_mod, b_mod)

    w_in_b = jnp.concatenate([_pair_heads(w_in[:, :, :D_ATTN], 2), w_in[:, :, D_ATTN:]], axis=2).astype(BF16)
    w_out_b = jnp.concatenate([_pair_heads(w_out[:, :D_ATTN, :], 1), w_out[:, D_ATTN:, :]], axis=1).astype(BF16)
    ffn_gu_b = ffn_w_gu.astype(BF16)
    ffn_down_b = ffn_w_down.astype(BF16)
    moe_gu_b = [_to_bf16(moe_w_gu, e) for e in range(DEPTH // 2)]
    moe_down_b = [_to_bf16(moe_w_down, e) for e in range(DEPTH // 2)]
    w_router = jnp.zeros((DEPTH // 2, D_MODEL, ROUTE_LANES), F32).at[:, :, :N_EXPERTS].set(moe_w_router)
    b_router = jnp.zeros((DEPTH // 2, 1, ROUTE_LANES), F32).at[:, 0, :N_EXPERTS].set(moe_b_router)
    sink2 = attn_sink * LOG2E
    cache_k4 = cache_k.reshape(dec_batch, DEPTH, past, D_KV)
    cache_v4 = cache_v.reshape(dec_batch, DEPTH, past, D_KV)
    rope_tabs = _rope_tables(dec_seq)
    gavg = jnp.kron(jnp.eye(CONV_GROUPS, dtype=F32),
                    jnp.full((D_CONV // CONV_GROUPS,) * 2, CONV_GROUPS / D_CONV, F32)).astype(BF16)
    tri = jnp.tril(jnp.ones((TOKEN_TILE, TOKEN_TILE), F32), -1).astype(BF16)
    conv_args = (conv_w, conv_b[:, None, :], gavg, conv_norm_g[:, None, :], conv_norm_b[:, None, :])

    def layer(l, x, mods_l, latent):
        n_seq, seq_len = (dec_batch, dec_seq) if latent else (batch, seq)
        row = lambda v: v[l][None, :]
        q, k, v, z = _inproj(x, mods_l, row(g_pre_mix), w_in_b, l, rope_tabs if latent else None, seq_len)
        if latent:
            a, cm = _mixers_latent(q, k, v, z, cache_k4, cache_v4, sink2[l], conv_args, l, n_seq, seq_len)
        else:
            a, cm = _mixers_context(q, k, v, z, sink2[l], conv_args, l, n_seq, seq_len)
        if l % 2 == 0:
            x1, h = _outproj(a, cm, x, mods_l, row(g_post_mix), row(g_pre_ffn), w_out_b, l, None)
            x2 = _ffn_dense(h, x1, mods_l, row(g_post_ffn), ffn_gu_b, ffn_down_b, l // 2)
        else:
            x1, h, route, counts = _outproj(a, cm, x, mods_l, row(g_post_mix), row(g_pre_ffn), w_out_b, l,
                                            (w_router[l // 2], b_router[l // 2], tri))
            x2 = _moe(h, x1, route, counts, mods_l, row(g_post_ffn), moe_gu_b[l // 2], moe_down_b[l // 2])
        return x2, k, v

    xp = x_prompt.reshape(batch * seq, D_MODEL)
    ks, vs = [], []
    for l in range(DEPTH):
        xp, k_l, v_l = layer(l, xp, mods[l, 0:1], False)
        ks.append(k_l.reshape(batch, seq, N_KV_HEADS, HEAD_DIM))
        vs.append(v_l.reshape(batch, seq, N_KV_HEADS, HEAD_DIM))

    xs = x_sample.reshape(dec_batch * dec_seq, D_MODEL)
    for l in range(DEPTH):
        xs, _, _ = layer(l, xs, mods[l, 1:1 + dec_batch], True)

    return (xp.reshape(batch, seq, D_MODEL), xs.reshape(dec_batch, dec_seq, D_MODEL),
            jnp.stack(ks, axis=1), jnp.stack(vs, axis=1))
```
